```python
import numpy as np
import jax
import jax.numpy as jnp
from jax import lax

D_MODEL = 1024
BATCH = 8
SEQ = 8192
DEPTH = 1

GRID_W = 64
NA_HEAD_DIM = 64
NA_HEADS = D_MODEL // 128
NA_WIDTH = NA_HEADS * NA_HEAD_DIM
NA_KH_MAX = 8
NA_KW = 16
HG_DK = 128
HG_DV = 128
HG_HEADS = D_MODEL // 256
HG_WIDTH = HG_HEADS * HG_DK
HG_VWIDTH = HG_HEADS * HG_DV
HG_CHUNK = 64
N_GROUPS = 4
EXPERTS_PER_GROUP = 8
N_EXPERTS = N_GROUPS * EXPERTS_PER_GROUP
TOP_K = 2
D_FF_EXPERT = D_MODEL // 2
MOE_BLOCK = 256
DN_ALPHA = (2 * DEPTH) ** 0.25
DN_BETA = (8 * DEPTH) ** -0.25
LN_EPS = 1e-5
RMS_EPS = 1e-6
IN_SIZES = (NA_WIDTH, NA_WIDTH, NA_WIDTH, HG_WIDTH, HG_WIDTH, HG_WIDTH, HG_VWIDTH, HG_VWIDTH, D_MODEL, D_MODEL)
D_IN = sum(IN_SIZES)

kernel_name = 'hybrid_na_hgrn2_hmoe_encoder'


def layer_norm(x, g, b):
    xf = x.astype(jnp.float32)
    mu = jnp.mean(xf, axis=-1, keepdims=True)
    xc = xf - mu
    var = jnp.mean(xc * xc, axis=-1, keepdims=True)
    return (xc * lax.rsqrt(var + LN_EPS) * g.astype(jnp.float32) + b.astype(jnp.float32)).astype(x.dtype)


def neighbourhood_attention(q, k, v, rpb):
    b, s, h, dh = q.shape
    rows = s // GRID_W
    kh = min(NA_KH_MAX, rows)

    def to_grid(t):
        return t.reshape(b, rows, GRID_W, h, dh).transpose(0, 3, 1, 2, 4)

    qg = to_grid(q * (dh ** -0.5))
    kg = to_grid(k)
    vg = to_grid(v)
    cols = np.arange(GRID_W)
    col_start = np.clip(cols - NA_KW // 2, 0, GRID_W - NA_KW)
    col_idx = col_start[:, None] + np.arange(NA_KW)[None, :]
    col_off = col_idx - cols[:, None] + (NA_KW - 1)

    def row_block(r):
        rs = jnp.clip(r - kh // 2, 0, rows - kh)
        q_row = lax.dynamic_index_in_dim(qg, r, axis=2, keepdims=False)
        k_band = lax.dynamic_slice_in_dim(kg, rs, kh, axis=2)
        v_band = lax.dynamic_slice_in_dim(vg, rs, kh, axis=2)
        k_win = k_band[:, :, :, col_idx]
        v_win = v_band[:, :, :, col_idx]
        row_off = rs + jnp.arange(kh) - r + (NA_KH_MAX - 1)
        bias = rpb[:, row_off][:, :, col_off].transpose(0, 2, 1, 3)
        scores = jnp.einsum('bhcd,bhrckd->bhcr k'.replace(' ', ''), q_row, k_win).astype(jnp.float32)
        scores = scores + bias[None].astype(jnp.float32)
        p = jax.nn.softmax(scores.reshape(b, h, GRID_W, kh * NA_KW), axis=-1)
        p = p.reshape(b, h, GRID_W, kh, NA_KW).astype(v.dtype)
        return jnp.einsum('bhcrk,bhrckd->bhcd', p, v_win)

    out = lax.map(row_block, jnp.arange(rows))
    return out.transpose(1, 0, 3, 2, 4).reshape(b, s, h * dh)


def gla_chunk_scan(q, k, v, log_f):
    _, b, n, c, dk = q.shape
    dv = v.shape[-1]
    causal = jnp.tril(jnp.ones((c, c), dtype=bool))

    def step(state, inp):
        qc, kc, vc, gc = inp
        bcum = jnp.cumsum(gc, axis=2)
        b_last = bcum[:, :, -1:, :]
        o_inter = jnp.einsum('bntk,bnkv->bntv', qc * jnp.exp(bcum), state)
        diff = bcum[:, :, :, None, :] - bcum[:, :, None, :, :]
        decay = jnp.exp(jnp.where(causal[:, :, None], diff, -jnp.inf))
        attn = jnp.einsum('bntk,bnsk,bntsk->bnts', qc, kc, decay)
        o_intra = jnp.einsum('bnts,bnsv->bntv', attn, vc)
        new_state = jnp.exp(b_last[:, :, 0, :])[..., None] * state + jnp.einsum(
            'bnsk,bnsv->bnkv', kc * jnp.exp(b_last - bcum), vc)
        return new_state, o_inter + o_intra

    init = jnp.zeros((b, n, dk, dv), jnp.float32)
    _, o = lax.scan(step, init, (q, k, v, log_f))
    return o


def hgrn2_bidirectional(q, f_fwd, f_bwd, i, g, lb, norm_g):
    b, s, h, dk = q.shape
    dv = i.shape[-1]
    f32 = jnp.float32
    qf = jax.nn.silu(q.astype(f32))
    lbr = lb.astype(f32).reshape(2, 1, 1, h, dk)
    forget = lbr + (1.0 - lbr) * jax.nn.sigmoid(jnp.stack([f_fwd, f_bwd]).astype(f32))
    log_f = jnp.log(forget)
    kk = 1.0 - forget
    vf = i.astype(f32)

    def rev(t):
        return jnp.flip(t, axis=1)

    qd = jnp.stack([qf, rev(qf)])
    kd = jnp.stack([kk[0], rev(kk[1])])
    vd = jnp.stack([vf, rev(vf)])
    gd = jnp.stack([log_f[0], rev(log_f[1])])
    nc = s // HG_CHUNK

    def to_chunks(t):
        d = t.shape[-1]
        return t.reshape(2, b, nc, HG_CHUNK, h, d).transpose(2, 1, 0, 4, 3, 5).reshape(nc, b, 2 * h, HG_CHUNK, d)

    o = gla_chunk_scan(to_chunks(qd), to_chunks(kd), to_chunks(vd), to_chunks(gd))
    o = o.reshape(nc, b, 2, h, HG_CHUNK, dv).transpose(2, 1, 0, 4, 3, 5).reshape(2, b, s, h, dv)
    o = o[0] + rev(o[1])
    o = o * lax.rsqrt(jnp.mean(o * o, axis=-1, keepdims=True) + RMS_EPS)
    o = o * norm_g.astype(f32).reshape(h, dv) * jax.nn.silu(g.astype(f32))
    return o.reshape(b, s, h * dv).astype(i.dtype)


def hierarchical_moe(x2d, w_rg, b_rg, w_re, b_re, w_gate, w_up, w_down):
    t, d = x2d.shape
    f32 = jnp.float32
    xf = x2d.astype(f32)
    g_logits = xf @ w_rg.astype(f32) + b_rg.astype(f32)
    g_prob = jax.nn.softmax(g_logits, axis=-1)
    g_sel = jnp.argmax(g_logits, axis=-1)
    g_w = jnp.take_along_axis(g_prob, g_sel[:, None], axis=-1)
    e_logits = (xf @ w_re.astype(f32) + b_re.astype(f32)).reshape(t, N_GROUPS, EXPERTS_PER_GROUP)
    e_logits = jnp.take_along_axis(e_logits, g_sel[:, None, None], axis=1)[:, 0]
    top_p, top_i = lax.top_k(jax.nn.softmax(e_logits, axis=-1), TOP_K)
    weights = g_w * top_p / jnp.sum(top_p, axis=-1, keepdims=True)
    expert_id = (g_sel[:, None] * EXPERTS_PER_GROUP + top_i).reshape(-1)
    n_assign = t * TOP_K
    order = jnp.argsort(expert_id)
    sorted_e = expert_id[order]
    counts = jnp.bincount(expert_id, length=N_EXPERTS)
    padded = (counts + MOE_BLOCK - 1) // MOE_BLOCK * MOE_BLOCK
    starts = jnp.cumsum(counts) - counts
    pends = jnp.cumsum(padded)
    pstarts = pends - padded
    dest = pstarts[sorted_e] + jnp.arange(n_assign) - starts[sorted_e]
    n_blocks = -(-n_assign // MOE_BLOCK) + N_EXPERTS
    tok = order // TOP_K
    buf_tok = jnp.zeros((n_blocks * MOE_BLOCK,), tok.dtype).at[dest].set(tok)
    block_expert = jnp.minimum(
        jnp.searchsorted(pends, jnp.arange(n_blocks) * MOE_BLOCK, side='right'), N_EXPERTS - 1)
    xbuf = x2d[buf_tok].reshape(n_blocks, MOE_BLOCK, d)

    def expert_block(args):
        xb, e = args
        hid = jax.nn.silu(xb @ w_gate[e]) * (xb @ w_up[e])
        return hid @ w_down[e]

    ybuf = lax.map(expert_block, (xbuf, block_expert)).reshape(-1, d)
    w_sorted = weights.reshape(-1)[order].astype(x2d.dtype)
    return jnp.zeros_like(x2d).at[tok].add(w_sorted[:, None] * ybuf[dest])


def setup_inputs(seed: int = 0) -> dict:
    key = jax.random.key(seed)
    ks = jax.random.split(key, 24)
    f32 = jnp.float32

    def nrm(k, shape, scale):
        return jax.random.normal(k, shape, f32) * scale

    col_scale = np.concatenate(
        [np.full((n,), DN_BETA if j in (2, 6) else 1.0, np.float32) for j, n in enumerate(IN_SIZES)])
    return {
        'x': nrm(ks[0], (BATCH, SEQ, D_MODEL), 1.0),
        'emb_ln_g': 1.0 + nrm(ks[1], (D_MODEL,), 0.05),
        'emb_ln_b': nrm(ks[2], (D_MODEL,), 0.02),
        'w_in': nrm(ks[3], (DEPTH, D_MODEL, D_IN), D_MODEL ** -0.5) * jnp.asarray(col_scale),
        'na_rpb': nrm(ks[4], (DEPTH, NA_HEADS, 2 * NA_KH_MAX - 1, 2 * NA_KW - 1), 0.02),
        'hg_lb': 1.0 + nrm(ks[5], (2, DEPTH + 1, HG_WIDTH), 0.1),
        'hg_norm_g': 1.0 + nrm(ks[6], (DEPTH, HG_VWIDTH), 0.05),
        'w_proj_a': nrm(ks[7], (DEPTH, NA_WIDTH, D_MODEL), NA_WIDTH ** -0.5 * DN_BETA),
        'w_proj_b': nrm(ks[8], (DEPTH, HG_VWIDTH, D_MODEL), HG_VWIDTH ** -0.5 * DN_BETA),
        'w_out': nrm(ks[9], (DEPTH, D_MODEL, D_MODEL), D_MODEL ** -0.5 * DN_BETA),
        'ln1_g': 1.0 + nrm(ks[10], (DEPTH, D_MODEL), 0.05),
        'ln1_b': nrm(ks[11], (DEPTH, D_MODEL), 0.02),
        'w_router_group': nrm(ks[12], (DEPTH, D_MODEL, N_GROUPS), D_MODEL ** -0.5),
        'b_router_group': nrm(ks[13], (DEPTH, N_GROUPS), 0.01),
        'w_router_expert': nrm(ks[14], (DEPTH, D_MODEL, N_EXPERTS), D_MODEL ** -0.5),
        'b_router_expert': nrm(ks[15], (DEPTH, N_EXPERTS), 0.01),
        'w_gate': nrm(ks[16], (DEPTH, N_EXPERTS, D_MODEL, D_FF_EXPERT), D_MODEL ** -0.5 * DN_BETA),
        'w_up': nrm(ks[17], (DEPTH, N_EXPERTS, D_MODEL, D_FF_EXPERT), D_MODEL ** -0.5 * DN_BETA),
        'w_down': nrm(ks[18], (DEPTH, N_EXPERTS, D_FF_EXPERT, D_MODEL), D_FF_EXPERT ** -0.5 * DN_BETA),
        'ln2_g': 1.0 + nrm(ks[19], (DEPTH, D_MODEL), 0.05),
        'ln2_b': nrm(ks[20], (DEPTH, D_MODEL), 0.02),
    }


def reference(x, emb_ln_g, emb_ln_b, w_in, na_rpb, hg_lb, hg_norm_g, w_proj_a, w_proj_b, w_out,
              ln1_g, ln1_b, w_router_group, b_router_group, w_router_expert, b_router_expert,
              w_gate, w_up, w_down, ln2_g, ln2_b):
    b, s, d = x.shape
    split_points = np.cumsum(IN_SIZES)[:-1].tolist()
    lb_all = jnp.cumsum(jax.nn.softmax(hg_lb.astype(jnp.float32), axis=1), axis=1)
    h = layer_norm(x, emb_ln_g, emb_ln_b)
    for l in range(DEPTH):
        u = h @ w_in[l]
        (na_q, na_k, na_v, hg_q, hg_ff, hg_fb, hg_i, hg_g, gate_a, gate_b) = jnp.split(u, split_points, axis=-1)
        a = neighbourhood_attention(na_q.reshape(b, s, NA_HEADS, NA_HEAD_DIM),
                                    na_k.reshape(b, s, NA_HEADS, NA_HEAD_DIM),
                                    na_v.reshape(b, s, NA_HEADS, NA_HEAD_DIM), na_rpb[l])
        c = hgrn2_bidirectional(hg_q.reshape(b, s, HG_HEADS, HG_DK),
                                hg_ff.reshape(b, s, HG_HEADS, HG_DK),
                                hg_fb.reshape(b, s, HG_HEADS, HG_DK),
                                hg_i.reshape(b, s, HG_HEADS, HG_DV),
                                hg_g.reshape(b, s, HG_HEADS, HG_DV),
                                lb_all[:, l], hg_norm_g[l])
        merged = jax.nn.sigmoid(gate_a) * (a @ w_proj_a[l]) + jax.nn.sigmoid(gate_b) * (c @ w_proj_b[l])
        h = layer_norm(DN_ALPHA * h + merged @ w_out[l], ln1_g[l], ln1_b[l])
        moe = hierarchical_moe(h.reshape(b * s, d), w_router_group[l], b_router_group[l],
                               w_router_expert[l], b_router_expert[l], w_gate[l], w_up[l], w_down[l])
        h = layer_norm(DN_ALPHA * h + moe.reshape(b, s, d), ln2_g[l], ln2_b[l])
    return h
```

```python
import functools

import numpy as np
import jax
import jax.numpy as jnp
from jax import lax
from jax.experimental import pallas as pl
from jax.experimental.pallas import tpu as pltpu

F32 = jnp.float32
BF16 = jnp.bfloat16

GRID_W = 64
NA_HEAD_DIM = 64
NA_KH = 8
NA_KW = 16
HG_DK = 128
N_GROUPS = 4
EXPERTS_PER_GROUP = 8
N_EXPERTS = N_GROUPS * EXPERTS_PER_GROUP
TOP_K = 2
LN_EPS = 1e-5
RMS_EPS = 1e-6

LANES = 128
VMEM_LIMIT = 56 * 1024 * 1024
NEG_BIG = -1e30

NA_QROWS = 4
NA_KROWS = NA_QROWS + NA_KH
HG_CHUNK = 64
HG_LEVELS = 6
MOE_BM = 256


def _ln(x, g, b):
    mu = jnp.mean(x, axis=-1, keepdims=True)
    xc = x - mu
    var = jnp.mean(xc * xc, axis=-1, keepdims=True)
    return xc * lax.rsqrt(var + LN_EPS) * g + b


def _sigmoid(x):
    return 1.0 / (1.0 + jnp.exp(-x))


def _inproj_kernel(x_ref, g_ref, b_ref, w_ref, na_ref, hg_ref, rest_ref, *, n_na, n_hg, q_cols, q_scale, cn):
    h = _ln(x_ref[...], g_ref[...], b_ref[...]).astype(BF16)
    d_in = w_ref.shape[1]
    for c0 in range(0, d_in, cn):
        u = jnp.dot(h, w_ref[:, c0:c0 + cn], preferred_element_type=F32)
        if c0 < q_cols:
            u = u * q_scale
        if c0 < n_na:
            na_ref[:, c0:c0 + cn] = u.astype(na_ref.dtype)
        elif c0 < n_na + n_hg:
            hg_ref[:, c0 - n_na:c0 - n_na + cn] = u
        else:
            o = c0 - n_na - n_hg
            rest_ref[:, o:o + cn] = u.astype(rest_ref.dtype)


def _inproj(x2d, g, b, w_bf16, *, n_na, n_hg, q_cols, q_scale, tm=512, cn=512):
    t, d = x2d.shape
    d_in = w_bf16.shape[1]
    n_rest = d_in - n_na - n_hg
    kern = functools.partial(_inproj_kernel, n_na=n_na, n_hg=n_hg, q_cols=q_cols, q_scale=q_scale, cn=cn)
    return pl.pallas_call(
        kern,
        grid=(t // tm,),
        in_specs=[
            pl.BlockSpec((tm, d), lambda i: (i, 0)),
            pl.BlockSpec((1, d), lambda i: (0, 0)),
            pl.BlockSpec((1, d), lambda i: (0, 0)),
            pl.BlockSpec((d, d_in), lambda i: (0, 0), pipeline_mode=pl.Buffered(1)),
        ],
        out_specs=[
            pl.BlockSpec((tm, n_na), lambda i: (i, 0)),
            pl.BlockSpec((tm, n_hg), lambda i: (i, 0)),
            pl.BlockSpec((tm, n_rest), lambda i: (i, 0)),
        ],
        out_shape=[
            jax.ShapeDtypeStruct((t, n_na), BF16),
            jax.ShapeDtypeStruct((t, n_hg), F32),
            jax.ShapeDtypeStruct((t, n_rest), BF16),
        ],
        compiler_params=pltpu.CompilerParams(
            dimension_semantics=("parallel",), vmem_limit_bytes=VMEM_LIMIT),
        name="ln_inproj",
    )(x2d, g.reshape(1, d), b.reshape(1, d), w_bf16)


def _na_bias_tables(rpb, rows):
    nq = NA_QROWS * GRID_W
    nk = NA_KROWS * GRID_W
    qi = np.arange(nq)
    kj = np.arange(nk)
    qc = (qi % GRID_W)[:, None]
    kc = (kj % GRID_W)[None, :]
    cs = np.clip(qc - NA_KW // 2, 0, GRID_W - NA_KW)
    valid_c = (kc >= cs) & (kc < cs + NA_KW)
    cidx = np.clip(kc - qc + NA_KW - 1, 0, 2 * NA_KW - 2)
    ridx_all, valid_all = [], []
    for r0 in (0, NA_QROWS, rows - NA_QROWS):
        start = int(np.clip(r0 - NA_KH // 2, 0, rows - NA_KROWS))
        qr = (r0 + qi // GRID_W)[:, None]
        kr = (start + kj // GRID_W)[None, :]
        rs = np.clip(qr - NA_KH // 2, 0, rows - NA_KH)
        valid_r = (kr >= rs) & (kr < rs + NA_KH)
        ridx_all.append(np.clip(kr - qr + NA_KH - 1, 0, 2 * NA_KH - 2) + 0 * cidx)
        valid_all.append(valid_r & valid_c)
    ridx = np.stack(ridx_all)
    valid = np.stack(valid_all)
    cidx3 = np.broadcast_to(cidx, ridx.shape)
    bias = rpb.astype(F32)[:, ridx, cidx3]
    bias = jnp.where(valid[None], bias, NEG_BIG)
    return bias.transpose(1, 0, 2, 3)


def _na_kernel(q_ref, k_ref, v_ref, bias_ref, o_ref, *, rows):
    rb = pl.program_id(2)
    start = jnp.clip(rb * NA_QROWS - NA_KH // 2, 0, rows - NA_KROWS)
    kstart = pl.multiple_of(start * GRID_W, NA_QROWS * GRID_W)
    nk = NA_KROWS * GRID_W
    kwin = k_ref[0, pl.ds(kstart, nk), :]
    vwin = v_ref[0, pl.ds(kstart, nk), :]
    q = q_ref[0]
    lane = lax.broadcasted_iota(jnp.int32, (1, LANES), 1)
    first = lane < NA_HEAD_DIM
    outs = []
    for hh in range(2):
        sel = first if hh == 0 else jnp.logical_not(first)
        qh = jnp.where(sel, q, jnp.zeros_like(q))
        s = lax.dot_general(qh, kwin, (((1,), (1,)), ((), ())), preferred_element_type=F32)
        s = s + bias_ref[0, 0, hh]
        m = jnp.max(s, axis=-1, keepdims=True)
        p = jnp.exp(s - m)
        l = jnp.sum(p, axis=-1, keepdims=True)
        o = jnp.dot(p.astype(BF16), vwin, preferred_element_type=F32)
        outs.append(o / l)
    o_ref[0] = jnp.where(first, outs[0], outs[1]).astype(o_ref.dtype)


def _na(qkv, bias, *, batch, seq):
    rows = seq // GRID_W
    width = qkv.shape[-1] // 3
    pairs = width // LANES
    nq = NA_QROWS * GRID_W
    nk = NA_KROWS * GRID_W
    nrb = rows // NA_QROWS
    bias5 = bias.reshape(3, pairs, 2, nq, nk)

    def bias_map(b, p, r):
        cls = jnp.where(r == 0, 0, jnp.where(r == nrb - 1, 2, 1))
        return (cls, p, 0, 0, 0)

    return pl.pallas_call(
        functools.partial(_na_kernel, rows=rows),
        grid=(batch, pairs, nrb),
        in_specs=[
            pl.BlockSpec((1, nq, LANES), lambda b, p, r: (b, r, p)),
            pl.BlockSpec((1, seq, LANES), lambda b, p, r: (b, 0, pairs + p)),
            pl.BlockSpec((1, seq, LANES), lambda b, p, r: (b, 0, 2 * pairs + p)),
            pl.BlockSpec((1, 1, 2, nq, nk), bias_map),
        ],
        out_specs=pl.BlockSpec((1, nq, LANES), lambda b, p, r: (b, r, p)),
        out_shape=jax.ShapeDtypeStruct((batch, seq, width), BF16),
        compiler_params=pltpu.CompilerParams(
            dimension_semantics=("parallel", "parallel", "arbitrary"), vmem_limit_bytes=VMEM_LIMIT),
        name="neighbourhood_attention",
    )(qkv, qkv, qkv, bias5)


def _hg_tables():
    c, nl = HG_CHUNK, HG_LEVELS
    a = np.zeros((2, c * (nl + 1) + 8, c), np.float32)
    lvl = np.full((2, c, c), -1, np.int32)
    for rev in (0, 1):
        for t in range(c):
            if rev == 0:
                a[rev, t, :t + 1] = 1.0
            else:
                a[rev, t, t:] = 1.0
            lvl[rev, t, t] = nl
            for s in range(c):
                if (s < t and rev == 0) or (s > t and rev == 1):
                    lvl[rev, t, s] = int(np.floor(np.log2(t ^ s)))
            for l in range(nl):
                m = 1 << l
                mid = (t & ~(2 * m - 1)) + m
                upper = (t >> l) & 1
                row = c * (l + 1) + t
                if rev == 0:
                    lo, hi = (mid, t) if upper else (t + 1, mid - 1)
                else:
                    lo, hi = (mid, t - 1) if upper else (t, mid - 1)
                if hi >= lo:
                    a[rev, row, lo:hi + 1] = 1.0
        a[rev, c * (nl + 1):, :] = 1.0
    return a, lvl


def _hg_kernel(q_ref, f_ref, i_ref, lb_ref, a_ref, lvl_ref, o_ref, st_ref, *, heads, n_chunks):
    rev = pl.program_id(1)
    c, nl = HG_CHUNK, HG_LEVELS

    @pl.when(pl.program_id(2) == 0)
    def _():
        st_ref[...] = jnp.zeros_like(st_ref)

    a_mat = a_ref[0]
    lvl = lvl_ref[0]
    row = lax.broadcasted_iota(jnp.int32, (c, LANES), 0)
    nt = (((1,), (1,)), ((), ()))
    tn = (((0,), (0,)), ((), ()))

    def chunk(j, carry):
        cj = jnp.where(rev == 1, n_chunks - 1 - j, j)
        r0 = pl.multiple_of(cj * c, c)
        for h in range(heads):
            cols = slice(h * LANES, (h + 1) * LANES)
            hq = q_ref[pl.ds(r0, c), cols]
            q = hq * _sigmoid(hq)
            lb = lb_ref[0, :, cols]
            forget = lb + (1.0 - lb) * _sigmoid(f_ref[pl.ds(r0, c), cols])
            g = jnp.log(forget)
            kk = 1.0 - forget
            v = i_ref[pl.ds(r0, c), cols]
            g_hi = g.astype(BF16)
            g_lo = (g - g_hi.astype(F32)).astype(BF16)
            e = jnp.dot(a_mat, g_hi, preferred_element_type=F32) + jnp.dot(a_mat, g_lo, preferred_element_type=F32)
            b = e[0:c]
            btot = e[c * (nl + 1):c * (nl + 1) + 1]
            st = st_ref[h]
            o = lax.dot_general((q * jnp.exp(b)).astype(BF16), st.astype(BF16), nt, preferred_element_type=F32)
            qb = q.astype(BF16)
            kb = kk.astype(BF16)
            attn = jnp.where(lvl == nl, lax.dot_general(qb, kb, nt, preferred_element_type=F32), 0.0)
            for l in range(nl):
                q_side = ((row >> l) & 1) != rev
                x = (jnp.where(q_side, q, kk) * jnp.exp(e[c * (l + 1):c * (l + 2)])).astype(BF16)
                attn = jnp.where(lvl == l, lax.dot_general(x, x, nt, preferred_element_type=F32), attn)
            o = o + jnp.dot(attn.astype(BF16), v, preferred_element_type=F32)
            o_ref[0, pl.ds(r0, c), cols] = o
            khat = (kk * jnp.exp(btot - b)).astype(BF16)
            st_ref[h] = st * jnp.exp(btot) + lax.dot_general(v, khat, tn, preferred_element_type=F32)
        return carry

    lax.fori_loop(0, n_chunks, chunk, 0)


def _hgrn(hg, rest, lb, *, batch, seq, heads, cb=512):
    t = batch * seq
    w = heads * LANES
    nsb = seq // cb
    a_np, lvl_np = _hg_tables()
    a_mat = jnp.asarray(a_np, BF16)
    lvl = jnp.asarray(lvl_np)

    def rowblk(b, d, s):
        return b * nsb + jnp.where(d == 1, nsb - 1 - s, s)

    return pl.pallas_call(
        functools.partial(_hg_kernel, heads=heads, n_chunks=cb // HG_CHUNK),
        grid=(batch, 2, nsb),
        in_specs=[
            pl.BlockSpec((cb, w), lambda b, d, s: (rowblk(b, d, s), 0)),
            pl.BlockSpec((cb, w), lambda b, d, s: (rowblk(b, d, s), 1 + d)),
            pl.BlockSpec((cb, w), lambda b, d, s: (rowblk(b, d, s), 0)),
            pl.BlockSpec((1, 1, w), lambda b, d, s: (d, 0, 0)),
            pl.BlockSpec((1,) + a_np.shape[1:], lambda b, d, s: (d, 0, 0)),
            pl.BlockSpec((1,) + lvl_np.shape[1:], lambda b, d, s: (d, 0, 0)),
        ],
        out_specs=pl.BlockSpec((1, cb, w), lambda b, d, s: (d, rowblk(b, d, s), 0)),
        out_shape=jax.ShapeDtypeStruct((2, t, w), F32),
        scratch_shapes=[pltpu.VMEM((heads, LANES, LANES), F32)],
        compiler_params=pltpu.CompilerParams(
            dimension_semantics=("parallel", "parallel", "arbitrary"), vmem_limit_bytes=VMEM_LIMIT),
        name="hgrn2_scan",
    )(hg, hg, rest, lb.reshape(2, 1, w), a_mat, lvl)


def _first_index_of_max(x, idx):
    m = jnp.max(x, axis=0, keepdims=True)
    i = jnp.min(jnp.where(x == m, idx, EXPERTS_PER_GROUP), axis=0, keepdims=True)
    return m, i


def _merge_kernel(a_ref, o2_ref, g_ref, ga_ref, gb_ref, x_ref, ln0g_ref, ln0b_ref, ng_ref,
                  wa_ref, wb_ref, wo_ref, ln1g_ref, ln1b_ref, wr_ref, br_ref,
                  h1_ref, ri_ref, rw_ref, *, heads, alpha):
    o = o2_ref[0] + o2_ref[1]
    gate = g_ref[...].astype(F32)
    parts = []
    for h in range(heads):
        cols = slice(h * LANES, (h + 1) * LANES)
        oh = o[:, cols]
        ms = jnp.mean(oh * oh, axis=-1, keepdims=True)
        gh = gate[:, cols]
        parts.append((oh * lax.rsqrt(ms + RMS_EPS) * ng_ref[:, cols] * (gh * _sigmoid(gh))).astype(BF16))
    c = jnp.concatenate(parts, axis=-1)
    pa = jnp.dot(a_ref[...], wa_ref[...], preferred_element_type=F32)
    pb = jnp.dot(c, wb_ref[...], preferred_element_type=F32)
    merged = _sigmoid(ga_ref[...].astype(F32)) * pa + _sigmoid(gb_ref[...].astype(F32)) * pb
    y = jnp.dot(merged.astype(BF16), wo_ref[...], preferred_element_type=F32)
    h0 = _ln(x_ref[...], ln0g_ref[...], ln0b_ref[...])
    h1 = _ln(alpha * h0 + y, ln1g_ref[...], ln1b_ref[...])
    h1_ref[...] = h1

    logits = lax.dot_general(wr_ref[...], h1, (((1,), (1,)), ((), ())),
                             precision=lax.Precision.HIGHEST, preferred_element_type=F32) + br_ref[:, 0:1]
    n = logits.shape[1]
    idx = lax.broadcasted_iota(jnp.int32, (EXPERTS_PER_GROUP, n), 0)
    glog = jnp.where(idx < N_GROUPS, logits[0:EXPERTS_PER_GROUP], -jnp.inf)
    gmax, gsel = _first_index_of_max(glog, idx)
    gw = 1.0 / jnp.sum(jnp.exp(glog - gmax), axis=0, keepdims=True)
    elog = logits[EXPERTS_PER_GROUP:2 * EXPERTS_PER_GROUP]
    for gi in range(1, N_GROUPS):
        lo = EXPERTS_PER_GROUP * (gi + 1)
        elog = jnp.where(gsel == gi, logits[lo:lo + EXPERTS_PER_GROUP], elog)
    ee = jnp.exp(elog - jnp.max(elog, axis=0, keepdims=True))
    p = ee / jnp.sum(ee, axis=0, keepdims=True)
    p1, i1 = _first_index_of_max(p, idx)
    p2, i2 = _first_index_of_max(jnp.where(idx == i1, -1.0, p), idx)
    denom = p1 + p2
    e1 = gsel * EXPERTS_PER_GROUP + i1
    e2 = gsel * EXPERTS_PER_GROUP + i2
    ri_ref[...] = jnp.where(idx == 0, e1, jnp.where(idx == 1, e2, 0))
    rw_ref[...] = jnp.where(idx == 0, gw * p1 / denom, jnp.where(idx == 1, gw * p2 / denom, 0.0))


def _merge(a, o2, rest, x2d, ln0g, ln0b, norm_g, wa, wb, wo, ln1g, ln1b, wr_t, br_col, *, heads, alpha, tm=256):
    t, d = x2d.shape
    w = heads * LANES
    wa_w = a.shape[1]
    full = lambda shape: pl.BlockSpec(shape, lambda i: (0,) * len(shape))
    return pl.pallas_call(
        functools.partial(_merge_kernel, heads=heads, alpha=alpha),
        grid=(t // tm,),
        in_specs=[
            pl.BlockSpec((tm, wa_w), lambda i: (i, 0)),
            pl.BlockSpec((2, tm, w), lambda i: (0, i, 0)),
            pl.BlockSpec((tm, w), lambda i: (i, 1)),
            pl.BlockSpec((tm, d), lambda i: (i, 1)),
            pl.BlockSpec((tm, d), lambda i: (i, 2)),
            pl.BlockSpec((tm, d), lambda i: (i, 0)),
            full((1, d)), full((1, d)), full((1, w)),
            full(wa.shape), full(wb.shape), full(wo.shape),
            full((1, d)), full((1, d)),
            full(wr_t.shape), full(br_col.shape),
        ],
        out_specs=[
            pl.BlockSpec((tm, d), lambda i: (i, 0)),
            pl.BlockSpec((EXPERTS_PER_GROUP, tm), lambda i: (0, i)),
            pl.BlockSpec((EXPERTS_PER_GROUP, tm), lambda i: (0, i)),
        ],
        out_shape=[
            jax.ShapeDtypeStruct((t, d), F32),
            jax.ShapeDtypeStruct((EXPERTS_PER_GROUP, t), jnp.int32),
            jax.ShapeDtypeStruct((EXPERTS_PER_GROUP, t), F32),
        ],
        compiler_params=pltpu.CompilerParams(
            dimension_semantics=("parallel",), vmem_limit_bytes=VMEM_LIMIT),
        name="merge_outproj_router",
    )(a, o2, rest, rest, rest, x2d, ln0g.reshape(1, d), ln0b.reshape(1, d), norm_g.reshape(1, w),
      wa, wb, wo, ln1g.reshape(1, d), ln1b.reshape(1, d), wr_t, br_col)


def _expert_kernel(be_ref, nv_ref, tok_ref, asg_ref, w_ref, h_hbm, wg_ref, wu_ref, wd_ref, out_hbm,
                   xbuf, ybuf, sems):
    i = pl.program_id(0)
    nvalid = nv_ref[i]
    bm = xbuf.shape[0]

    def gather_copy(r, tok):
        return pltpu.make_async_copy(h_hbm.at[pl.ds(tok, 1), :], xbuf.at[pl.ds(r, 1), :], sems.at[0])

    def scatter_copy(r, dst):
        return pltpu.make_async_copy(ybuf.at[pl.ds(r, 1), :], out_hbm.at[pl.ds(dst, 1), :], sems.at[1])

    @pl.when(nvalid > 0)
    def _():
        def g_start(r, c):
            gather_copy(r, tok_ref[0, 0, r]).start()
            return c

        def g_wait(r, c):
            gather_copy(r, 0).wait()
            return c

        lax.fori_loop(0, bm, g_start, 0)
        lax.fori_loop(0, bm, g_wait, 0)
        x = xbuf[...].astype(BF16)
        gate = jnp.dot(x, wg_ref[0], preferred_element_type=F32)
        up = jnp.dot(x, wu_ref[0], preferred_element_type=F32)
        hid = (gate * _sigmoid(gate) * up).astype(BF16)
        ybuf[...] = jnp.dot(hid, wd_ref[0], preferred_element_type=F32) * w_ref[...]

        def s_start(r, c):
            scatter_copy(r, asg_ref[0, 0, r]).start()
            return c

        def s_wait(r, c):
            scatter_copy(r, 0).wait()
            return c

        lax.fori_loop(0, nvalid, s_start, 0)
        lax.fori_loop(0, nvalid, s_wait, 0)


def _experts(h1, block_expert, block_nvalid, tok, asg, w_col, wg, wu, wd, *, n_assign):
    t, d = h1.shape
    n_blocks, _, bm = tok.shape
    ne, _, dff = wg.shape
    return pl.pallas_call(
        _expert_kernel,
        grid_spec=pltpu.PrefetchScalarGridSpec(
            num_scalar_prefetch=2,
            grid=(n_blocks,),
            in_specs=[
                pl.BlockSpec((1, 1, bm), lambda i, be, nv: (i, 0, 0), memory_space=pltpu.SMEM),
                pl.BlockSpec((1, 1, bm), lambda i, be, nv: (i, 0, 0), memory_space=pltpu.SMEM),
                pl.BlockSpec((bm, 1), lambda i, be, nv: (i, 0)),
                pl.BlockSpec(memory_space=pl.ANY),
                pl.BlockSpec((1, d, dff), lambda i, be, nv: (be[i], 0, 0)),
                pl.BlockSpec((1, d, dff), lambda i, be, nv: (be[i], 0, 0)),
                pl.BlockSpec((1, dff, d), lambda i, be, nv: (be[i], 0, 0)),
            ],
            out_specs=pl.BlockSpec(memory_space=pl.ANY),
            scratch_shapes=[
                pltpu.VMEM((bm, d), F32),
                pltpu.VMEM((bm, d), F32),
                pltpu.SemaphoreType.DMA((2,)),
            ],
        ),
        out_shape=jax.ShapeDtypeStruct((n_assign, d), F32),
        compiler_params=pltpu.CompilerParams(
            dimension_semantics=("arbitrary",), vmem_limit_bytes=VMEM_LIMIT),
        name="expert_ffn",
    )(block_expert, block_nvalid, tok, asg, w_col, h1, wg, wu, wd)


def _dispatch_plan(route_i, route_w, *, bm):
    t = route_i.shape[1]
    n_assign = t * TOP_K
    eid = route_i[:TOP_K].T.reshape(-1)
    wts = route_w[:TOP_K].T.reshape(-1)
    order = jnp.argsort(eid).astype(jnp.int32)
    counts = jnp.bincount(eid, length=N_EXPERTS).astype(jnp.int32)
    padded = (counts + bm - 1) // bm * bm
    starts = jnp.cumsum(counts) - counts
    pends = jnp.cumsum(padded)
    pstarts = pends - padded
    n_blocks = n_assign // bm + N_EXPERTS
    block_expert = jnp.minimum(
        jnp.searchsorted(pends, jnp.arange(n_blocks, dtype=jnp.int32) * bm, side='right'),
        N_EXPERTS - 1).astype(jnp.int32)
    slot = jnp.arange(n_blocks * bm, dtype=jnp.int32)
    se = jnp.repeat(block_expert, bm)
    off = slot - pstarts[se]
    valid = (off >= 0) & (off < counts[se])
    src = jnp.where(valid, order[jnp.clip(starts[se] + off, 0, n_assign - 1)], -1)
    tok = jnp.where(valid, src // TOP_K, 0)
    w_col = jnp.where(valid, wts[jnp.maximum(src, 0)], 0.0)
    block_nvalid = jnp.sum(valid.reshape(n_blocks, bm), axis=1).astype(jnp.int32)
    return (block_expert, block_nvalid, tok.reshape(n_blocks, 1, bm), src.reshape(n_blocks, 1, bm),
            w_col.reshape(n_blocks * bm, 1))


def _final_kernel(h1_ref, y_ref, g_ref, b_ref, o_ref, *, alpha):
    d = h1_ref.shape[1]
    z = alpha * h1_ref[...] + y_ref[:, 0:d] + y_ref[:, d:2 * d]
    o_ref[...] = _ln(z, g_ref[...], b_ref[...])


def _final(h1, y2, g, b, *, alpha, tm=512):
    t, d = h1.shape
    return pl.pallas_call(
        functools.partial(_final_kernel, alpha=alpha),
        grid=(t // tm,),
        in_specs=[
            pl.BlockSpec((tm, d), lambda i: (i, 0)),
            pl.BlockSpec((tm, TOP_K * d), lambda i: (i, 0)),
            pl.BlockSpec((1, d), lambda i: (0, 0)),
            pl.BlockSpec((1, d), lambda i: (0, 0)),
        ],
        out_specs=pl.BlockSpec((tm, d), lambda i: (i, 0)),
        out_shape=jax.ShapeDtypeStruct((t, d), F32),
        compiler_params=pltpu.CompilerParams(
            dimension_semantics=("parallel",), vmem_limit_bytes=VMEM_LIMIT),
        name="residual_postnorm",
    )(h1, y2, g.reshape(1, d), b.reshape(1, d))


def kernel(x, emb_ln_g, emb_ln_b, w_in, na_rpb, hg_lb, hg_norm_g, w_proj_a, w_proj_b, w_out, ln1_g, ln1_b,
           w_router_group, b_router_group, w_router_expert, b_router_expert, w_gate, w_up, w_down, ln2_g, ln2_b):
    batch, seq, d = x.shape
    depth = w_in.shape[0]
    t = batch * seq
    na_w = na_rpb.shape[1] * NA_HEAD_DIM
    hg_w = hg_norm_g.shape[1]
    hg_heads = hg_w // HG_DK
    alpha = float((2 * depth) ** 0.25)
    lb_all = jnp.cumsum(jax.nn.softmax(hg_lb.astype(F32), axis=1), axis=1)

    assert depth == 1, depth
    l = 0
    x2d = x.reshape(t, d)
    na, hg, rest = _inproj(x2d, emb_ln_g, emb_ln_b, w_in[l].astype(BF16), n_na=3 * na_w, n_hg=3 * hg_w,
                           q_cols=na_w, q_scale=float(NA_HEAD_DIM ** -0.5))
    bias = _na_bias_tables(na_rpb[l], seq // GRID_W)
    a = _na(na.reshape(batch, seq, 3 * na_w), bias, batch=batch, seq=seq).reshape(t, na_w)
    o2 = _hgrn(hg, rest, lb_all[:, l], batch=batch, seq=seq, heads=hg_heads)

    wr_t = jnp.zeros((LANES, d), F32)
    wr_t = wr_t.at[0:N_GROUPS].set(w_router_group[l].T.astype(F32))
    wr_t = wr_t.at[EXPERTS_PER_GROUP:EXPERTS_PER_GROUP + N_EXPERTS].set(w_router_expert[l].T.astype(F32))
    br = jnp.zeros((LANES,), F32)
    br = br.at[0:N_GROUPS].set(b_router_group[l].astype(F32))
    br = br.at[EXPERTS_PER_GROUP:EXPERTS_PER_GROUP + N_EXPERTS].set(b_router_expert[l].astype(F32))
    br_col = jnp.broadcast_to(br[:, None], (LANES, LANES))

    h1, route_i, route_w = _merge(
        a, o2, rest, x2d, emb_ln_g, emb_ln_b, hg_norm_g[l], w_proj_a[l].astype(BF16), w_proj_b[l].astype(BF16),
        w_out[l].astype(BF16), ln1_g[l], ln1_b[l], wr_t, br_col, heads=hg_heads, alpha=alpha)

    plan = _dispatch_plan(route_i, route_w, bm=MOE_BM)
    y = _experts(h1, *plan, w_gate[l].astype(BF16), w_up[l].astype(BF16), w_down[l].astype(BF16),
                 n_assign=t * TOP_K)
    h2 = _final(h1, y.reshape(t, TOP_K * d), ln2_g[l], ln2_b[l], alpha=alpha)
    return h2.reshape(batch, seq, d)
```

```python
import functools

import numpy as np
import jax
import jax.numpy as jnp
from jax import lax
from jax.experimental import pallas as pl
from jax.experimental.pallas import tpu as pltpu

F32 = jnp.float32
BF16 = jnp.bfloat16

GRID_W = 64
NA_HEAD_DIM = 64
NA_KH = 8
NA_KW = 16
HG_DK = 128
N_GROUPS = 4
EXPERTS_PER_GROUP = 8
N_EXPERTS = N_GROUPS * EXPERTS_PER_GROUP
TOP_K = 2
LN_EPS = 1e-5
RMS_EPS = 1e-6

LANES = 128
VMEM_LIMIT = 56 * 1024 * 1024
NEG_BIG = -1e30

NA_QROWS = 4
NA_KROWS = NA_QROWS + NA_KH
HG_CHUNK = 64
HG_LEVELS = 6
MOE_TM = 256
MOE_CHUNK = 8
MOE_CAP = TOP_K * MOE_TM + N_EXPERTS * MOE_CHUNK
MOE_BM = 256
MOE_CPB = MOE_BM // MOE_CHUNK


def _ln(x, g, b):
    mu = jnp.mean(x, axis=-1, keepdims=True)
    xc = x - mu
    var = jnp.mean(xc * xc, axis=-1, keepdims=True)
    return xc * lax.rsqrt(var + LN_EPS) * g + b


def _sigmoid(x):
    return 1.0 / (1.0 + jnp.exp(-x))


def _inproj_kernel(x_ref, g_ref, b_ref, w_ref, na_ref, hg_ref, rest_ref, *, n_na, n_hg, q_cols, q_scale, cn):
    h = _ln(x_ref[...], g_ref[...], b_ref[...]).astype(BF16)
    d_in = w_ref.shape[1]
    for c0 in range(0, d_in, cn):
        u = jnp.dot(h, w_ref[:, c0:c0 + cn], preferred_element_type=F32)
        if c0 < q_cols:
            u = u * q_scale
        if c0 < n_na:
            na_ref[:, c0:c0 + cn] = u.astype(na_ref.dtype)
        elif c0 < n_na + n_hg:
            hg_ref[:, c0 - n_na:c0 - n_na + cn] = u
        else:
            o = c0 - n_na - n_hg
            rest_ref[:, o:o + cn] = u.astype(rest_ref.dtype)


def _inproj(x2d, g, b, w_bf16, *, n_na, n_hg, q_cols, q_scale, tm=512, cn=512):
    t, d = x2d.shape
    d_in = w_bf16.shape[1]
    n_rest = d_in - n_na - n_hg
    kern = functools.partial(_inproj_kernel, n_na=n_na, n_hg=n_hg, q_cols=q_cols, q_scale=q_scale, cn=cn)
    return pl.pallas_call(
        kern,
        grid=(t // tm,),
        in_specs=[
            pl.BlockSpec((tm, d), lambda i: (i, 0)),
            pl.BlockSpec((1, d), lambda i: (0, 0)),
            pl.BlockSpec((1, d), lambda i: (0, 0)),
            pl.BlockSpec((d, d_in), lambda i: (0, 0), pipeline_mode=pl.Buffered(1)),
        ],
        out_specs=[
            pl.BlockSpec((tm, n_na), lambda i: (i, 0)),
            pl.BlockSpec((tm, n_hg), lambda i: (i, 0)),
            pl.BlockSpec((tm, n_rest), lambda i: (i, 0)),
        ],
        out_shape=[
            jax.ShapeDtypeStruct((t, n_na), BF16),
            jax.ShapeDtypeStruct((t, n_hg), F32),
            jax.ShapeDtypeStruct((t, n_rest), BF16),
        ],
        compiler_params=pltpu.CompilerParams(
            dimension_semantics=("parallel",), vmem_limit_bytes=VMEM_LIMIT),
        name="ln_inproj",
    )(x2d, g.reshape(1, d), b.reshape(1, d), w_bf16)


def _na_bias_tables(rpb, rows):
    nq = NA_QROWS * GRID_W
    nk = NA_KROWS * GRID_W
    qi = np.arange(nq)
    kj = np.arange(nk)
    qc = (qi % GRID_W)[:, None]
    kc = (kj % GRID_W)[None, :]
    cs = np.clip(qc - NA_KW // 2, 0, GRID_W - NA_KW)
    valid_c = (kc >= cs) & (kc < cs + NA_KW)
    n_rel_r, n_rel_c = 2 * NA_KH - 1, 2 * NA_KW - 1
    wc = np.arange(GRID_W)
    cidx = np.clip(wc[None, :] - wc[:, None] + NA_KW - 1, 0, n_rel_c - 1)
    onehot_c = (cidx[None] == np.arange(n_rel_c)[:, None, None]).astype(np.float32)
    ridx_all, valid_all = [], []
    for r0 in (0, NA_QROWS, rows - NA_QROWS):
        start = int(np.clip(r0 - NA_KH // 2, 0, rows - NA_KROWS))
        qr = (r0 + qi // GRID_W)[:, None]
        kr = (start + kj // GRID_W)[None, :]
        rs = np.clip(qr - NA_KH // 2, 0, rows - NA_KH)
        valid_all.append((kr >= rs) & (kr < rs + NA_KH) & valid_c)
        qr_s = r0 + np.arange(NA_QROWS)[:, None]
        kr_s = start + np.arange(NA_KROWS)[None, :]
        ridx_all.append(np.clip(kr_s - qr_s + NA_KH - 1, 0, n_rel_r - 1))
    valid = np.stack(valid_all)
    rows_sel = rpb.astype(F32)[:, np.stack(ridx_all), :]
    bias = jnp.einsum('hcqkj,jxy->chqxky', rows_sel, onehot_c, precision=lax.Precision.HIGHEST)
    bias = bias.reshape(3, rpb.shape[0], nq, nk)
    return jnp.where(valid[:, None], bias, NEG_BIG)


def _na_kernel(q_ref, k_ref, v_ref, bias_ref, o_ref, *, rows):
    rb = pl.program_id(2)
    start = jnp.clip(rb * NA_QROWS - NA_KH // 2, 0, rows - NA_KROWS)
    kstart = pl.multiple_of(start * GRID_W, NA_QROWS * GRID_W)
    nk = NA_KROWS * GRID_W
    kwin = k_ref[0, pl.ds(kstart, nk), :]
    vwin = v_ref[0, pl.ds(kstart, nk), :]
    q = q_ref[0]
    lane = lax.broadcasted_iota(jnp.int32, (1, LANES), 1)
    first = lane < NA_HEAD_DIM
    outs = []
    for hh in range(2):
        sel = first if hh == 0 else jnp.logical_not(first)
        qh = jnp.where(sel, q, jnp.zeros_like(q))
        s = lax.dot_general(qh, kwin, (((1,), (1,)), ((), ())), preferred_element_type=F32)
        s = s + bias_ref[0, 0, hh]
        m = jnp.max(s, axis=-1, keepdims=True)
        p = jnp.exp(s - m)
        l = jnp.sum(p, axis=-1, keepdims=True)
        o = jnp.dot(p.astype(BF16), vwin, preferred_element_type=F32)
        outs.append(o / l)
    o_ref[0] = jnp.where(first, outs[0], outs[1]).astype(o_ref.dtype)


def _na(qkv, bias, *, batch, seq):
    rows = seq // GRID_W
    width = qkv.shape[-1] // 3
    pairs = width // LANES
    nq = NA_QROWS * GRID_W
    nk = NA_KROWS * GRID_W
    nrb = rows // NA_QROWS
    bias5 = bias.reshape(3, pairs, 2, nq, nk)

    def bias_map(b, p, r):
        cls = jnp.where(r == 0, 0, jnp.where(r == nrb - 1, 2, 1))
        return (cls, p, 0, 0, 0)

    return pl.pallas_call(
        functools.partial(_na_kernel, rows=rows),
        grid=(batch, pairs, nrb),
        in_specs=[
            pl.BlockSpec((1, nq, LANES), lambda b, p, r: (b, r, p)),
            pl.BlockSpec((1, seq, LANES), lambda b, p, r: (b, 0, pairs + p)),
            pl.BlockSpec((1, seq, LANES), lambda b, p, r: (b, 0, 2 * pairs + p)),
            pl.BlockSpec((1, 1, 2, nq, nk), bias_map),
        ],
        out_specs=pl.BlockSpec((1, nq, LANES), lambda b, p, r: (b, r, p)),
        out_shape=jax.ShapeDtypeStruct((batch, seq, width), BF16),
        compiler_params=pltpu.CompilerParams(
            dimension_semantics=("parallel", "parallel", "arbitrary"), vmem_limit_bytes=VMEM_LIMIT),
        name="neighbourhood_attention",
    )(qkv, qkv, qkv, bias5)


def _hg_tables():
    c, nl = HG_CHUNK, HG_LEVELS
    a = np.zeros((2, c * (nl + 1) + 8, c), np.float32)
    lvl = np.full((2, c, c), -1, np.int32)
    for rev in (0, 1):
        for t in range(c):
            if rev == 0:
                a[rev, t, :t + 1] = 1.0
            else:
                a[rev, t, t:] = 1.0
            lvl[rev, t, t] = nl
            for s in range(c):
                if (s < t and rev == 0) or (s > t and rev == 1):
                    lvl[rev, t, s] = int(np.floor(np.log2(t ^ s)))
            for l in range(nl):
                m = 1 << l
                mid = (t & ~(2 * m - 1)) + m
                upper = (t >> l) & 1
                row = c * (l + 1) + t
                if rev == 0:
                    lo, hi = (mid, t) if upper else (t + 1, mid - 1)
                else:
                    lo, hi = (mid, t - 1) if upper else (t, mid - 1)
                if hi >= lo:
                    a[rev, row, lo:hi + 1] = 1.0
        a[rev, c * (nl + 1):, :] = 1.0
    return a, lvl


def _hg_kernel(q_ref, f_ref, i_ref, lb_ref, a_ref, lvl_ref, o_ref, st_ref, *, heads, n_chunks):
    rev = pl.program_id(1)
    c, nl = HG_CHUNK, HG_LEVELS

    @pl.when(pl.program_id(2) == 0)
    def _():
        st_ref[...] = jnp.zeros_like(st_ref)

    a_mat = a_ref[0]
    lvl = lvl_ref[0]
    row = lax.broadcasted_iota(jnp.int32, (c, LANES), 0)
    nt = (((1,), (1,)), ((), ()))
    tn = (((0,), (0,)), ((), ()))

    def chunk(j, carry):
        cj = jnp.where(rev == 1, n_chunks - 1 - j, j)
        r0 = pl.multiple_of(cj * c, c)
        for h in range(heads):
            cols = slice(h * LANES, (h + 1) * LANES)
            hq = q_ref[pl.ds(r0, c), cols]
            q = hq * _sigmoid(hq)
            lb = lb_ref[0, :, cols]
            forget = lb + (1.0 - lb) * _sigmoid(f_ref[pl.ds(r0, c), cols])
            g = jnp.log(forget)
            kk = 1.0 - forget
            v = i_ref[pl.ds(r0, c), cols]
            g_hi = g.astype(BF16)
            g_lo = (g - g_hi.astype(F32)).astype(BF16)
            e = jnp.dot(a_mat, g_hi, preferred_element_type=F32) + jnp.dot(a_mat, g_lo, preferred_element_type=F32)
            b = e[0:c]
            btot = e[c * (nl + 1):c * (nl + 1) + 1]
            st = st_ref[h]
            o = lax.dot_general((q * jnp.exp(b)).astype(BF16), st.astype(BF16), nt, preferred_element_type=F32)
            qb = q.astype(BF16)
            kb = kk.astype(BF16)
            attn = jnp.where(lvl == nl, lax.dot_general(qb, kb, nt, preferred_element_type=F32), 0.0)
            for l in range(nl):
                q_side = ((row >> l) & 1) != rev
                x = (jnp.where(q_side, q, kk) * jnp.exp(e[c * (l + 1):c * (l + 2)])).astype(BF16)
                attn = jnp.where(lvl == l, lax.dot_general(x, x, nt, preferred_element_type=F32), attn)
            o = o + jnp.dot(attn.astype(BF16), v, preferred_element_type=F32)
            o_ref[0, pl.ds(r0, c), cols] = o
            khat = (kk * jnp.exp(btot - b)).astype(BF16)
            st_ref[h] = st * jnp.exp(btot) + lax.dot_general(v, khat, tn, preferred_element_type=F32)
        return carry

    lax.fori_loop(0, n_chunks, chunk, 0)


def _hgrn(hg, rest, lb, *, batch, seq, heads, cb=512):
    t = batch * seq
    w = heads * LANES
    nsb = seq // cb
    a_np, lvl_np = _hg_tables()
    a_mat = jnp.asarray(a_np, BF16)
    lvl = jnp.asarray(lvl_np)

    def rowblk(b, d, s):
        return b * nsb + jnp.where(d == 1, nsb - 1 - s, s)

    return pl.pallas_call(
        functools.partial(_hg_kernel, heads=heads, n_chunks=cb // HG_CHUNK),
        grid=(batch, 2, nsb),
        in_specs=[
            pl.BlockSpec((cb, w), lambda b, d, s: (rowblk(b, d, s), 0)),
            pl.BlockSpec((cb, w), lambda b, d, s: (rowblk(b, d, s), 1 + d)),
            pl.BlockSpec((cb, w), lambda b, d, s: (rowblk(b, d, s), 0)),
            pl.BlockSpec((1, 1, w), lambda b, d, s: (d, 0, 0)),
            pl.BlockSpec((1,) + a_np.shape[1:], lambda b, d, s: (d, 0, 0)),
            pl.BlockSpec((1,) + lvl_np.shape[1:], lambda b, d, s: (d, 0, 0)),
        ],
        out_specs=pl.BlockSpec((1, cb, w), lambda b, d, s: (d, rowblk(b, d, s), 0)),
        out_shape=jax.ShapeDtypeStruct((2, t, w), F32),
        scratch_shapes=[pltpu.VMEM((heads, LANES, LANES), F32)],
        compiler_params=pltpu.CompilerParams(
            dimension_semantics=("parallel", "parallel", "arbitrary"), vmem_limit_bytes=VMEM_LIMIT),
        name="hgrn2_scan",
    )(hg, hg, rest, lb.reshape(2, 1, w), a_mat, lvl)


def _first_index_of_max(x, idx):
    m = jnp.max(x, axis=0, keepdims=True)
    i = jnp.min(jnp.where(x == m, idx, EXPERTS_PER_GROUP), axis=0, keepdims=True)
    return m, i


def _merge_kernel(a_ref, o2_ref, g_ref, ga_ref, gb_ref, x_ref, ln0g_ref, ln0b_ref, ng_ref,
                  wa_ref, wb_ref, wo_ref, ln1g_ref, ln1b_ref, wr_ref, br_ref,
                  h1_ref, ri_ref, rw_ref, xs_ref, pc_ref, *, heads, alpha):
    o = o2_ref[0] + o2_ref[1]
    gate = g_ref[...].astype(F32)
    parts = []
    for h in range(heads):
        cols = slice(h * LANES, (h + 1) * LANES)
        oh = o[:, cols]
        ms = jnp.mean(oh * oh, axis=-1, keepdims=True)
        gh = gate[:, cols]
        parts.append((oh * lax.rsqrt(ms + RMS_EPS) * ng_ref[:, cols] * (gh * _sigmoid(gh))).astype(BF16))
    c = jnp.concatenate(parts, axis=-1)
    pa = jnp.dot(a_ref[...], wa_ref[...], preferred_element_type=F32)
    pb = jnp.dot(c, wb_ref[...], preferred_element_type=F32)
    merged = _sigmoid(ga_ref[...].astype(F32)) * pa + _sigmoid(gb_ref[...].astype(F32)) * pb
    y = jnp.dot(merged.astype(BF16), wo_ref[...], preferred_element_type=F32)
    h0 = _ln(x_ref[...], ln0g_ref[...], ln0b_ref[...])
    h1 = _ln(alpha * h0 + y, ln1g_ref[...], ln1b_ref[...])
    h1_ref[...] = h1

    logits = lax.dot_general(wr_ref[...], h1, (((1,), (1,)), ((), ())),
                             precision=lax.Precision.HIGHEST, preferred_element_type=F32) + br_ref[:, 0:1]
    n = logits.shape[1]
    idx = lax.broadcasted_iota(jnp.int32, (EXPERTS_PER_GROUP, n), 0)
    glog = jnp.where(idx < N_GROUPS, logits[0:EXPERTS_PER_GROUP], -jnp.inf)
    gmax, gsel = _first_index_of_max(glog, idx)
    gw = 1.0 / jnp.sum(jnp.exp(glog - gmax), axis=0, keepdims=True)
    elog = logits[EXPERTS_PER_GROUP:2 * EXPERTS_PER_GROUP]
    for gi in range(1, N_GROUPS):
        lo = EXPERTS_PER_GROUP * (gi + 1)
        elog = jnp.where(gsel == gi, logits[lo:lo + EXPERTS_PER_GROUP], elog)
    ee = jnp.exp(elog - jnp.max(elog, axis=0, keepdims=True))
    p = ee / jnp.sum(ee, axis=0, keepdims=True)
    p1, i1 = _first_index_of_max(p, idx)
    p2, i2 = _first_index_of_max(jnp.where(idx == i1, -1.0, p), idx)
    denom = p1 + p2
    e1 = gsel * EXPERTS_PER_GROUP + i1
    e2 = gsel * EXPERTS_PER_GROUP + i2
    rw_ref[...] = jnp.where(idx == 0, gw * p1 / denom, jnp.where(idx == 1, gw * p2 / denom, 0.0))

    ex = lax.broadcasted_iota(jnp.int32, (N_EXPERTS, n), 0)
    m1 = ex == e1
    m2 = ex == e2
    member = jnp.where(m1, 1.0, jnp.where(m2, 1.0, 0.0))
    before = (lax.broadcasted_iota(jnp.int32, (n, n), 0) < lax.broadcasted_iota(jnp.int32, (n, n), 1))
    rank = jnp.dot(member.astype(BF16), jnp.where(before, 1.0, 0.0).astype(BF16), preferred_element_type=F32)
    count = jnp.sum(member, axis=1, keepdims=True)
    chunks = jnp.floor((count + (MOE_CHUNK - 1)) * (1.0 / MOE_CHUNK))
    chunks_b = jnp.broadcast_to(chunks, (N_EXPERTS, LANES))
    lower = (lax.broadcasted_iota(jnp.int32, (N_EXPERTS, N_EXPERTS), 1)
             < lax.broadcasted_iota(jnp.int32, (N_EXPERTS, N_EXPERTS), 0))
    seg = jnp.dot(jnp.where(lower, 1.0, 0.0), chunks_b, precision=lax.Precision.HIGHEST,
                  preferred_element_type=F32)
    posmat = seg[:, 0:1] * MOE_CHUNK + rank
    pos1 = jnp.sum(jnp.where(m1, posmat, 0.0), axis=0, keepdims=True).astype(jnp.int32)
    pos2 = jnp.sum(jnp.where(m2, posmat, 0.0), axis=0, keepdims=True).astype(jnp.int32)
    ri_ref[...] = jnp.where(idx == 0, pos1, jnp.where(idx == 1, pos2, 0))
    pc_ref[...] = chunks_b
    slot = lax.broadcasted_iota(jnp.int32, (xs_ref.shape[0], n), 0)
    perm = jnp.where(slot == pos1, 1.0, jnp.where(slot == pos2, 1.0, 0.0)).astype(BF16)
    xs_ref[...] = jnp.dot(perm, h1.astype(BF16), preferred_element_type=F32)


def _merge(a, o2, rest, x2d, ln0g, ln0b, norm_g, wa, wb, wo, ln1g, ln1b, wr_t, br_col, *, heads, alpha):
    t, d = x2d.shape
    w = heads * LANES
    wa_w = a.shape[1]
    tm = MOE_TM
    n_tiles = t // tm
    full = lambda shape: pl.BlockSpec(shape, lambda i: (0,) * len(shape))
    return pl.pallas_call(
        functools.partial(_merge_kernel, heads=heads, alpha=alpha),
        grid=(t // tm,),
        in_specs=[
            pl.BlockSpec((tm, wa_w), lambda i: (i, 0)),
            pl.BlockSpec((2, tm, w), lambda i: (0, i, 0)),
            pl.BlockSpec((tm, w), lambda i: (i, 1)),
            pl.BlockSpec((tm, d), lambda i: (i, 1)),
            pl.BlockSpec((tm, d), lambda i: (i, 2)),
            pl.BlockSpec((tm, d), lambda i: (i, 0)),
            full((1, d)), full((1, d)), full((1, w)),
            full(wa.shape), full(wb.shape), full(wo.shape),
            full((1, d)), full((1, d)),
            full(wr_t.shape), full(br_col.shape),
        ],
        out_specs=[
            pl.BlockSpec((tm, d), lambda i: (i, 0)),
            pl.BlockSpec((EXPERTS_PER_GROUP, tm), lambda i: (0, i)),
            pl.BlockSpec((EXPERTS_PER_GROUP, tm), lambda i: (0, i)),
            pl.BlockSpec((MOE_CAP, d), lambda i: (i, 0)),
            pl.BlockSpec((N_EXPERTS, LANES), lambda i: (i, 0)),
        ],
        out_shape=[
            jax.ShapeDtypeStruct((t, d), F32),
            jax.ShapeDtypeStruct((EXPERTS_PER_GROUP, t), jnp.int32),
            jax.ShapeDtypeStruct((EXPERTS_PER_GROUP, t), F32),
            jax.ShapeDtypeStruct((n_tiles * MOE_CAP, d), F32),
            jax.ShapeDtypeStruct((n_tiles * N_EXPERTS, LANES), F32),
        ],
        compiler_params=pltpu.CompilerParams(
            dimension_semantics=("parallel",), vmem_limit_bytes=VMEM_LIMIT),
        name="merge_outproj_router",
    )(a, o2, rest, rest, rest, x2d, ln0g.reshape(1, d), ln0b.reshape(1, d), norm_g.reshape(1, w),
      wa, wb, wo, ln1g.reshape(1, d), ln1b.reshape(1, d), wr_t, br_col)


def _expert_kernel(be_ref, nv_ref, cur_ref, nxt_ref, xy_in, wg_ref, wu_ref, wd_ref, xy_out,
                   xbuf, ybuf, gsem, ssem):
    i = pl.program_id(0)
    n = pl.num_programs(0)
    slot = i % 2
    nv = nv_ref[i]

    def gather_copy(tbl_ref, s, buf_slot):
        src = pl.multiple_of(tbl_ref[0, 0, s] * MOE_CHUNK, MOE_CHUNK)
        dst = pl.multiple_of(s * MOE_CHUNK, MOE_CHUNK)
        return pltpu.make_async_copy(xy_in.at[pl.ds(src, MOE_CHUNK), :],
                                     xbuf.at[buf_slot, pl.ds(dst, MOE_CHUNK), :], gsem.at[buf_slot])

    def scatter_copy(s, buf_slot):
        src = pl.multiple_of(s * MOE_CHUNK, MOE_CHUNK)
        dst = pl.multiple_of(cur_ref[0, 0, s] * MOE_CHUNK, MOE_CHUNK)
        return pltpu.make_async_copy(ybuf.at[buf_slot, pl.ds(src, MOE_CHUNK), :],
                                     xy_out.at[pl.ds(dst, MOE_CHUNK), :], ssem.at[buf_slot])

    def start_gathers(tbl_ref, buf_slot):
        def body(s, c):
            gather_copy(tbl_ref, s, buf_slot).start()
            return c
        lax.fori_loop(0, MOE_CPB, body, 0)

    def wait_scatters(count, buf_slot):
        def body(s, c):
            scatter_copy(0, buf_slot).wait()
            return c
        lax.fori_loop(0, count, body, 0)

    @pl.when((i == 0) & (nv > 0))
    def _():
        start_gathers(cur_ref, slot)

    nxt = jnp.minimum(i + 1, n - 1)

    @pl.when((i + 1 < n) & (nv_ref[nxt] > 0))
    def _():
        start_gathers(nxt_ref, 1 - slot)

    @pl.when(i >= 2)
    def _():
        wait_scatters(nv_ref[jnp.maximum(i - 2, 0)], slot)

    @pl.when(nv > 0)
    def _():
        def g_wait(s, c):
            gather_copy(cur_ref, 0, slot).wait()
            return c
        lax.fori_loop(0, MOE_CPB, g_wait, 0)

        x = xbuf[slot].astype(BF16)
        gate = jnp.dot(x, wg_ref[0], preferred_element_type=F32)
        up = jnp.dot(x, wu_ref[0], preferred_element_type=F32)
        hid = (gate * _sigmoid(gate) * up).astype(BF16)
        ybuf[slot] = jnp.dot(hid, wd_ref[0], preferred_element_type=F32)

        def s_start(s, c):
            scatter_copy(s, slot).start()
            return c
        lax.fori_loop(0, nv, s_start, 0)

    @pl.when(i == n - 1)
    def _():
        @pl.when(i >= 1)
        def _():
            wait_scatters(nv_ref[jnp.maximum(i - 1, 0)], 1 - slot)
        wait_scatters(nv, slot)


def _experts(xy, block_expert, block_nvalid, table, wg, wu, wd):
    rows, d = xy.shape
    n_blocks = table.shape[0]
    ne, _, dff = wg.shape
    return pl.pallas_call(
        _expert_kernel,
        grid_spec=pltpu.PrefetchScalarGridSpec(
            num_scalar_prefetch=2,
            grid=(n_blocks,),
            in_specs=[
                pl.BlockSpec((1, 1, MOE_CPB), lambda i, be, nv: (i, 0, 0), memory_space=pltpu.SMEM),
                pl.BlockSpec((1, 1, MOE_CPB), lambda i, be, nv: (jnp.minimum(i + 1, n_blocks - 1), 0, 0),
                             memory_space=pltpu.SMEM),
                pl.BlockSpec(memory_space=pl.ANY),
                pl.BlockSpec((1, d, dff), lambda i, be, nv: (be[i], 0, 0)),
                pl.BlockSpec((1, d, dff), lambda i, be, nv: (be[i], 0, 0)),
                pl.BlockSpec((1, dff, d), lambda i, be, nv: (be[i], 0, 0)),
            ],
            out_specs=pl.BlockSpec(memory_space=pl.ANY),
            scratch_shapes=[
                pltpu.VMEM((2, MOE_BM, d), F32),
                pltpu.VMEM((2, MOE_BM, d), F32),
                pltpu.SemaphoreType.DMA((2,)),
                pltpu.SemaphoreType.DMA((2,)),
            ],
        ),
        out_shape=jax.ShapeDtypeStruct((rows, d), F32),
        input_output_aliases={4: 0},
        compiler_params=pltpu.CompilerParams(
            dimension_semantics=("arbitrary",), vmem_limit_bytes=VMEM_LIMIT),
        name="expert_ffn",
    )(block_expert, block_nvalid, table, table, xy, wg, wu, wd)


def _chunk_plan(tile_chunks):
    n_tiles = tile_chunks.shape[0]
    cap_chunks = MOE_CAP // MOE_CHUNK
    max_tile_chunks = (TOP_K * MOE_TM + N_EXPERTS * (MOE_CHUNK - 1)) // MOE_CHUNK
    n_blocks = -(-n_tiles * max_tile_chunks // MOE_CPB) + N_EXPERTS
    pc = tile_chunks.astype(jnp.int32)
    seg = jnp.cumsum(pc, axis=1) - pc
    run_len = pc.T.reshape(-1)
    run_src = (jnp.arange(n_tiles, dtype=jnp.int32)[None, :] * cap_chunks + seg.T).reshape(-1)
    run_end = jnp.cumsum(run_len)
    run_beg = run_end - run_len
    tot = jnp.sum(pc, axis=0)
    nblk = (tot + MOE_CPB - 1) // MOE_CPB
    bend = jnp.cumsum(nblk)
    bbeg = bend - nblk
    cbeg = jnp.cumsum(tot) - tot
    bidx = jnp.arange(n_blocks, dtype=jnp.int32)
    be = jnp.minimum(jnp.searchsorted(bend, bidx, side='right'), N_EXPERTS - 1).astype(jnp.int32)
    q = ((bidx - bbeg[be]) * MOE_CPB)[:, None] + jnp.arange(MOE_CPB, dtype=jnp.int32)[None, :]
    valid = (q < tot[be][:, None]) & (bidx < bend[-1])[:, None]
    g = cbeg[be][:, None] + q
    run = jnp.minimum(jnp.searchsorted(run_end, g.reshape(-1), side='right'), run_len.shape[0] - 1)
    run = run.reshape(g.shape)
    src = run_src[run] + (g - run_beg[run])
    src = jnp.where(valid, src, src[:, 0:1])
    nvalid = jnp.sum(valid, axis=1).astype(jnp.int32)
    src = jnp.where(nvalid[:, None] > 0, src, 0).astype(jnp.int32)
    return be, nvalid, src.reshape(n_blocks, 1, MOE_CPB)


def _final_kernel(h1_ref, ys_ref, pos_ref, w_ref, g_ref, b_ref, o_ref, *, alpha):
    cap = ys_ref.shape[0]
    n = h1_ref.shape[0]
    slot = lax.broadcasted_iota(jnp.int32, (cap, n), 0)
    pw = (jnp.where(slot == pos_ref[0:1, :], w_ref[0:1, :], 0.0)
          + jnp.where(slot == pos_ref[1:2, :], w_ref[1:2, :], 0.0))
    pw_hi = pw.astype(BF16)
    pw_lo = (pw - pw_hi.astype(F32)).astype(BF16)
    ys = ys_ref[...].astype(BF16)
    tn = (((0,), (0,)), ((), ()))
    moe = (lax.dot_general(pw_hi, ys, tn, preferred_element_type=F32)
           + lax.dot_general(pw_lo, ys, tn, preferred_element_type=F32))
    o_ref[...] = _ln(alpha * h1_ref[...] + moe, g_ref[...], b_ref[...])


def _final(h1, ys, pos, wts, g, b, *, alpha):
    t, d = h1.shape
    tm = MOE_TM
    return pl.pallas_call(
        functools.partial(_final_kernel, alpha=alpha),
        grid=(t // tm,),
        in_specs=[
            pl.BlockSpec((tm, d), lambda i: (i, 0)),
            pl.BlockSpec((MOE_CAP, d), lambda i: (i, 0)),
            pl.BlockSpec((EXPERTS_PER_GROUP, tm), lambda i: (0, i)),
            pl.BlockSpec((EXPERTS_PER_GROUP, tm), lambda i: (0, i)),
            pl.BlockSpec((1, d), lambda i: (0, 0)),
            pl.BlockSpec((1, d), lambda i: (0, 0)),
        ],
        out_specs=pl.BlockSpec((tm, d), lambda i: (i, 0)),
        out_shape=jax.ShapeDtypeStruct((t, d), F32),
        compiler_params=pltpu.CompilerParams(
            dimension_semantics=("parallel",), vmem_limit_bytes=VMEM_LIMIT),
        name="combine_postnorm",
    )(h1, ys, pos, wts, g.reshape(1, d), b.reshape(1, d))


def kernel(x, emb_ln_g, emb_ln_b, w_in, na_rpb, hg_lb, hg_norm_g, w_proj_a, w_proj_b, w_out, ln1_g, ln1_b,
           w_router_group, b_router_group, w_router_expert, b_router_expert, w_gate, w_up, w_down, ln2_g, ln2_b):
    batch, seq, d = x.shape
    depth = w_in.shape[0]
    t = batch * seq
    na_w = na_rpb.shape[1] * NA_HEAD_DIM
    hg_w = hg_norm_g.shape[1]
    hg_heads = hg_w // HG_DK
    alpha = float((2 * depth) ** 0.25)
    lb_all = jnp.cumsum(jax.nn.softmax(hg_lb.astype(F32), axis=1), axis=1)

    assert depth == 1, depth
    l = 0
    x2d = x.reshape(t, d)
    na, hg, rest = _inproj(x2d, emb_ln_g, emb_ln_b, w_in[l].astype(BF16), n_na=3 * na_w, n_hg=3 * hg_w,
                           q_cols=na_w, q_scale=float(NA_HEAD_DIM ** -0.5))
    bias = _na_bias_tables(na_rpb[l], seq // GRID_W)
    a = _na(na.reshape(batch, seq, 3 * na_w), bias, batch=batch, seq=seq).reshape(t, na_w)
    o2 = _hgrn(hg, rest, lb_all[:, l], batch=batch, seq=seq, heads=hg_heads)

    wr_t = jnp.zeros((LANES, d), F32)
    wr_t = wr_t.at[0:N_GROUPS].set(w_router_group[l].T.astype(F32))
    wr_t = wr_t.at[EXPERTS_PER_GROUP:EXPERTS_PER_GROUP + N_EXPERTS].set(w_router_expert[l].T.astype(F32))
    br = jnp.zeros((LANES,), F32)
    br = br.at[0:N_GROUPS].set(b_router_group[l].astype(F32))
    br = br.at[EXPERTS_PER_GROUP:EXPERTS_PER_GROUP + N_EXPERTS].set(b_router_expert[l].astype(F32))
    br_col = jnp.broadcast_to(br[:, None], (LANES, LANES))

    h1, pos, route_w, xs, tile_chunks = _merge(
        a, o2, rest, x2d, emb_ln_g, emb_ln_b, hg_norm_g[l], w_proj_a[l].astype(BF16), w_proj_b[l].astype(BF16),
        w_out[l].astype(BF16), ln1_g[l], ln1_b[l], wr_t, br_col, heads=hg_heads, alpha=alpha)

    plan = _chunk_plan(tile_chunks[:, 0].reshape(t // MOE_TM, N_EXPERTS))
    ys = _experts(xs, *plan, w_gate[l].astype(BF16), w_up[l].astype(BF16), w_down[l].astype(BF16))
    h2 = _final(h1, ys, pos, route_w, ln2_g[l], ln2_b[l], alpha=alpha)
    return h2.reshape(batch, seq, d)
```

```python
import functools

import numpy as np
import jax
import jax.numpy as jnp
from jax import lax
from jax.experimental import pallas as pl
from jax.experimental.pallas import tpu as pltpu

F32 = jnp.float32
BF16 = jnp.bfloat16

GRID_W = 64
NA_HEAD_DIM = 64
NA_KH = 8
NA_KW = 16
HG_DK = 128
N_GROUPS = 4
EXPERTS_PER_GROUP = 8
N_EXPERTS = N_GROUPS * EXPERTS_PER_GROUP
TOP_K = 2
LN_EPS = 1e-5
RMS_EPS = 1e-6

LANES = 128
VMEM_LIMIT = 56 * 1024 * 1024
NEG_BIG = -1e30

NA_QROWS = 4
NA_KROWS = NA_QROWS + NA_KH
HG_CHUNK = 128
HG_LEVELS = 7
MOE_TM = 256
MOE_CHUNK = 8
MOE_CAP = TOP_K * MOE_TM + N_EXPERTS * MOE_CHUNK
MOE_BM = 256
MOE_CPB = MOE_BM // MOE_CHUNK


def _ln(x, g, b):
    mu = jnp.mean(x, axis=-1, keepdims=True)
    xc = x - mu
    var = jnp.mean(xc * xc, axis=-1, keepdims=True)
    return xc * lax.rsqrt(var + LN_EPS) * g + b


def _sigmoid(x):
    return 1.0 / (1.0 + jnp.exp(-x))


def _inproj_kernel(x_ref, g_ref, b_ref, w_ref, na_ref, hg_ref, rest_ref, *, n_na, n_hg, q_cols, q_scale, cn):
    h = _ln(x_ref[...], g_ref[...], b_ref[...]).astype(BF16)
    d_in = w_ref.shape[1]
    for c0 in range(0, d_in, cn):
        u = jnp.dot(h, w_ref[:, c0:c0 + cn], preferred_element_type=F32)
        if c0 < q_cols:
            u = u * q_scale
        if c0 < n_na:
            na_ref[:, c0:c0 + cn] = u.astype(na_ref.dtype)
        elif c0 < n_na + n_hg:
            hg_ref[:, c0 - n_na:c0 - n_na + cn] = u
        else:
            o = c0 - n_na - n_hg
            rest_ref[:, o:o + cn] = u.astype(rest_ref.dtype)


def _inproj(x2d, g, b, w_bf16, *, n_na, n_hg, q_cols, q_scale, tm=512, cn=512):
    t, d = x2d.shape
    d_in = w_bf16.shape[1]
    n_rest = d_in - n_na - n_hg
    kern = functools.partial(_inproj_kernel, n_na=n_na, n_hg=n_hg, q_cols=q_cols, q_scale=q_scale, cn=cn)
    return pl.pallas_call(
        kern,
        grid=(t // tm,),
        in_specs=[
            pl.BlockSpec((tm, d), lambda i: (i, 0)),
            pl.BlockSpec((1, d), lambda i: (0, 0)),
            pl.BlockSpec((1, d), lambda i: (0, 0)),
            pl.BlockSpec((d, d_in), lambda i: (0, 0), pipeline_mode=pl.Buffered(1)),
        ],
        out_specs=[
            pl.BlockSpec((tm, n_na), lambda i: (i, 0)),
            pl.BlockSpec((tm, n_hg), lambda i: (i, 0)),
            pl.BlockSpec((tm, n_rest), lambda i: (i, 0)),
        ],
        out_shape=[
            jax.ShapeDtypeStruct((t, n_na), BF16),
            jax.ShapeDtypeStruct((t, n_hg), F32),
            jax.ShapeDtypeStruct((t, n_rest), BF16),
        ],
        compiler_params=pltpu.CompilerParams(
            dimension_semantics=("parallel",), vmem_limit_bytes=VMEM_LIMIT),
        name="ln_inproj",
    )(x2d, g.reshape(1, d), b.reshape(1, d), w_bf16)


def _na_bias_tables(rpb, rows):
    nq = NA_QROWS * GRID_W
    nk = NA_KROWS * GRID_W
    qi = np.arange(nq)
    kj = np.arange(nk)
    qc = (qi % GRID_W)[:, None]
    kc = (kj % GRID_W)[None, :]
    cs = np.clip(qc - NA_KW // 2, 0, GRID_W - NA_KW)
    valid_c = (kc >= cs) & (kc < cs + NA_KW)
    n_rel_r, n_rel_c = 2 * NA_KH - 1, 2 * NA_KW - 1
    wc = np.arange(GRID_W)
    cidx = np.clip(wc[None, :] - wc[:, None] + NA_KW - 1, 0, n_rel_c - 1)
    onehot_c = (cidx[None] == np.arange(n_rel_c)[:, None, None]).astype(np.float32)
    ridx_all, valid_all = [], []
    for r0 in (0, NA_QROWS, rows - NA_QROWS):
        start = int(np.clip(r0 - NA_KH // 2, 0, rows - NA_KROWS))
        qr = (r0 + qi // GRID_W)[:, None]
        kr = (start + kj // GRID_W)[None, :]
        rs = np.clip(qr - NA_KH // 2, 0, rows - NA_KH)
        valid_all.append((kr >= rs) & (kr < rs + NA_KH) & valid_c)
        qr_s = r0 + np.arange(NA_QROWS)[:, None]
        kr_s = start + np.arange(NA_KROWS)[None, :]
        ridx_all.append(np.clip(kr_s - qr_s + NA_KH - 1, 0, n_rel_r - 1))
    valid = np.stack(valid_all)
    rows_sel = rpb.astype(F32)[:, np.stack(ridx_all), :]
    bias = jnp.einsum('hcqkj,jxy->chqxky', rows_sel, onehot_c, precision=lax.Precision.HIGHEST)
    bias = bias.reshape(3, rpb.shape[0], nq, nk)
    return jnp.where(valid[:, None], bias, NEG_BIG)


def _na_kernel(q_ref, k_ref, v_ref, bias_ref, o_ref, *, rows):
    rb = pl.program_id(2)
    start = jnp.clip(rb * NA_QROWS - NA_KH // 2, 0, rows - NA_KROWS)
    kstart = pl.multiple_of(start * GRID_W, NA_QROWS * GRID_W)
    nk = NA_KROWS * GRID_W
    kwin = k_ref[0, pl.ds(kstart, nk), :]
    vwin = v_ref[0, pl.ds(kstart, nk), :]
    q = q_ref[0]
    lane = lax.broadcasted_iota(jnp.int32, (1, LANES), 1)
    first = lane < NA_HEAD_DIM
    outs = []
    for hh in range(2):
        sel = first if hh == 0 else jnp.logical_not(first)
        qh = jnp.where(sel, q, jnp.zeros_like(q))
        s = lax.dot_general(qh, kwin, (((1,), (1,)), ((), ())), preferred_element_type=F32)
        s = s + bias_ref[0, 0, hh]
        m = jnp.max(s, axis=-1, keepdims=True)
        p = jnp.exp(s - m)
        l = jnp.sum(p, axis=-1, keepdims=True)
        o = jnp.dot(p.astype(BF16), vwin, preferred_element_type=F32)
        outs.append(o / l)
    o_ref[0] = jnp.where(first, outs[0], outs[1]).astype(o_ref.dtype)


def _na(qkv, bias, *, batch, seq):
    rows = seq // GRID_W
    width = qkv.shape[-1] // 3
    pairs = width // LANES
    nq = NA_QROWS * GRID_W
    nk = NA_KROWS * GRID_W
    nrb = rows // NA_QROWS
    bias5 = bias.reshape(3, pairs, 2, nq, nk)

    def bias_map(b, p, r):
        cls = jnp.where(r == 0, 0, jnp.where(r == nrb - 1, 2, 1))
        return (cls, p, 0, 0, 0)

    return pl.pallas_call(
        functools.partial(_na_kernel, rows=rows),
        grid=(batch, pairs, nrb),
        in_specs=[
            pl.BlockSpec((1, nq, LANES), lambda b, p, r: (b, r, p)),
            pl.BlockSpec((1, seq, LANES), lambda b, p, r: (b, 0, pairs + p)),
            pl.BlockSpec((1, seq, LANES), lambda b, p, r: (b, 0, 2 * pairs + p)),
            pl.BlockSpec((1, 1, 2, nq, nk), bias_map),
        ],
        out_specs=pl.BlockSpec((1, nq, LANES), lambda b, p, r: (b, r, p)),
        out_shape=jax.ShapeDtypeStruct((batch, seq, width), BF16),
        compiler_params=pltpu.CompilerParams(
            dimension_semantics=("parallel", "parallel", "arbitrary"), vmem_limit_bytes=VMEM_LIMIT),
        name="neighbourhood_attention",
    )(qkv, qkv, qkv, bias5)


def _hg_tables():
    c, nl = HG_CHUNK, HG_LEVELS
    a = np.zeros((2, c * (nl + 1) + 8, c), np.float32)
    lvl = np.full((2, c, c), -1, np.int32)
    for rev in (0, 1):
        for t in range(c):
            if rev == 0:
                a[rev, t, :t + 1] = 1.0
            else:
                a[rev, t, t:] = 1.0
            lvl[rev, t, t] = nl
            for s in range(c):
                if (s < t and rev == 0) or (s > t and rev == 1):
                    lvl[rev, t, s] = int(np.floor(np.log2(t ^ s)))
            for l in range(nl):
                m = 1 << l
                mid = (t & ~(2 * m - 1)) + m
                upper = (t >> l) & 1
                row = c * (l + 1) + t
                if rev == 0:
                    lo, hi = (mid, t) if upper else (t + 1, mid - 1)
                else:
                    lo, hi = (mid, t - 1) if upper else (t, mid - 1)
                if hi >= lo:
                    a[rev, row, lo:hi + 1] = 1.0
        a[rev, c * (nl + 1):, :] = 1.0
    return a, lvl


def _hg_kernel(q_ref, f_ref, i_ref, lb_ref, a_ref, lvl_ref, o_ref, st_ref, *, heads, n_chunks):
    rev = pl.program_id(1)
    c, nl = HG_CHUNK, HG_LEVELS

    @pl.when(pl.program_id(2) == 0)
    def _():
        st_ref[...] = jnp.zeros_like(st_ref)

    a_mat = a_ref[0]
    lvl = lvl_ref[0]
    w = heads * LANES
    row = lax.broadcasted_iota(jnp.int32, (c, w), 0)
    nt = (((1,), (1,)), ((), ()))
    tn = (((0,), (0,)), ((), ()))
    head_cols = [slice(h * LANES, (h + 1) * LANES) for h in range(heads)]

    def chunk(j, carry):
        cj = jnp.where(rev == 1, n_chunks - 1 - j, j)
        r0 = pl.multiple_of(cj * c, c)
        hq = q_ref[pl.ds(r0, c), :]
        q = hq * _sigmoid(hq)
        lb = lb_ref[0]
        forget = lb + (1.0 - lb) * _sigmoid(f_ref[pl.ds(r0, c), :])
        g = jnp.log(forget)
        kk = 1.0 - forget
        v = i_ref[pl.ds(r0, c), :]
        g_hi = g.astype(BF16)
        g_lo = (g - g_hi.astype(F32)).astype(BF16)
        e = jnp.dot(a_mat, jnp.concatenate([g_hi, g_lo], axis=0), preferred_element_type=F32)
        b = e[0:c]
        btot = e[c * (nl + 1):c * (nl + 1) + 1]
        qd = (q * jnp.exp(b)).astype(BF16)
        khat = (kk * jnp.exp(btot - b)).astype(BF16)
        decay = jnp.exp(btot)
        xs = [q.astype(BF16)]
        for l in range(nl):
            q_side = ((row >> l) & 1) != rev
            xs.append((jnp.where(q_side, q, kk) * jnp.exp(e[c * (l + 1):c * (l + 2)])).astype(BF16))
        kb = kk.astype(BF16)
        sts = [st_ref[h] for h in range(heads)]
        outs = [lax.dot_general(qd[:, hc], sts[h].astype(BF16), nt, preferred_element_type=F32)
                for h, hc in enumerate(head_cols)]
        attns = [jnp.where(lvl == nl, lax.dot_general(xs[0][:, hc], kb[:, hc], nt, preferred_element_type=F32), 0.0)
                 for hc in head_cols]
        for l in range(nl):
            x = xs[l + 1]
            attns = [jnp.where(lvl == l, lax.dot_general(x[:, hc], x[:, hc], nt, preferred_element_type=F32), at)
                     for hc, at in zip(head_cols, attns)]
        for h, hc in enumerate(head_cols):
            o_ref[0, pl.ds(r0, c), hc] = outs[h] + jnp.dot(attns[h].astype(BF16), v[:, hc],
                                                           preferred_element_type=F32)
        for h, hc in enumerate(head_cols):
            st_ref[h] = sts[h] * decay[:, hc] + lax.dot_general(v[:, hc], khat[:, hc], tn,
                                                                 preferred_element_type=F32)
        return carry

    lax.fori_loop(0, n_chunks, chunk, 0)


def _hgrn(hg, rest, lb, *, batch, seq, heads, cb=512):
    t = batch * seq
    w = heads * LANES
    nsb = seq // cb
    a_np, lvl_np = _hg_tables()
    a_np = np.concatenate([a_np, a_np], axis=-1)
    a_mat = jnp.asarray(a_np, BF16)
    lvl = jnp.asarray(lvl_np)

    def rowblk(b, d, s):
        return b * nsb + jnp.where(d == 1, nsb - 1 - s, s)

    return pl.pallas_call(
        functools.partial(_hg_kernel, heads=heads, n_chunks=cb // HG_CHUNK),
        grid=(batch, 2, nsb),
        in_specs=[
            pl.BlockSpec((cb, w), lambda b, d, s: (rowblk(b, d, s), 0)),
            pl.BlockSpec((cb, w), lambda b, d, s: (rowblk(b, d, s), 1 + d)),
            pl.BlockSpec((cb, w), lambda b, d, s: (rowblk(b, d, s), 0)),
            pl.BlockSpec((1, 1, w), lambda b, d, s: (d, 0, 0)),
            pl.BlockSpec((1,) + a_np.shape[1:], lambda b, d, s: (d, 0, 0)),
            pl.BlockSpec((1,) + lvl_np.shape[1:], lambda b, d, s: (d, 0, 0)),
        ],
        out_specs=pl.BlockSpec((1, cb, w), lambda b, d, s: (d, rowblk(b, d, s), 0)),
        out_shape=jax.ShapeDtypeStruct((2, t, w), F32),
        scratch_shapes=[pltpu.VMEM((heads, LANES, LANES), F32)],
        compiler_params=pltpu.CompilerParams(
            dimension_semantics=("parallel", "parallel", "arbitrary"), vmem_limit_bytes=VMEM_LIMIT),
        name="hgrn2_scan",
    )(hg, hg, rest, lb.reshape(2, 1, w), a_mat, lvl)


def _first_index_of_max(x, idx):
    m = jnp.max(x, axis=0, keepdims=True)
    i = jnp.min(jnp.where(x == m, idx, EXPERTS_PER_GROUP), axis=0, keepdims=True)
    return m, i


def _merge_kernel(a_ref, o2_ref, g_ref, ga_ref, gb_ref, x_ref, ln0g_ref, ln0b_ref, ng_ref,
                  wa_ref, wb_ref, wo_ref, ln1g_ref, ln1b_ref, wr_ref, br_ref,
                  h1_ref, ri_ref, rw_ref, xs_ref, pc_ref, *, heads, alpha):
    o = o2_ref[0] + o2_ref[1]
    gate = g_ref[...].astype(F32)
    parts = []
    for h in range(heads):
        cols = slice(h * LANES, (h + 1) * LANES)
        oh = o[:, cols]
        ms = jnp.mean(oh * oh, axis=-1, keepdims=True)
        gh = gate[:, cols]
        parts.append((oh * lax.rsqrt(ms + RMS_EPS) * ng_ref[:, cols] * (gh * _sigmoid(gh))).astype(BF16))
    c = jnp.concatenate(parts, axis=-1)
    pa = jnp.dot(a_ref[...], wa_ref[...], preferred_element_type=F32)
    pb = jnp.dot(c, wb_ref[...], preferred_element_type=F32)
    merged = _sigmoid(ga_ref[...].astype(F32)) * pa + _sigmoid(gb_ref[...].astype(F32)) * pb
    y = jnp.dot(merged.astype(BF16), wo_ref[...], preferred_element_type=F32)
    h0 = _ln(x_ref[...], ln0g_ref[...], ln0b_ref[...])
    h1 = _ln(alpha * h0 + y, ln1g_ref[...], ln1b_ref[...])
    h1_ref[...] = h1

    logits = lax.dot_general(wr_ref[...], h1, (((1,), (1,)), ((), ())),
                             precision=lax.Precision.HIGHEST, preferred_element_type=F32) + br_ref[:, 0:1]
    n = logits.shape[1]
    idx = lax.broadcasted_iota(jnp.int32, (EXPERTS_PER_GROUP, n), 0)
    glog = jnp.where(idx < N_GROUPS, logits[0:EXPERTS_PER_GROUP], -jnp.inf)
    gmax, gsel = _first_index_of_max(glog, idx)
    gw = 1.0 / jnp.sum(jnp.exp(glog - gmax), axis=0, keepdims=True)
    elog = logits[EXPERTS_PER_GROUP:2 * EXPERTS_PER_GROUP]
    for gi in range(1, N_GROUPS):
        lo = EXPERTS_PER_GROUP * (gi + 1)
        elog = jnp.where(gsel == gi, logits[lo:lo + EXPERTS_PER_GROUP], elog)
    ee = jnp.exp(elog - jnp.max(elog, axis=0, keepdims=True))
    p = ee / jnp.sum(ee, axis=0, keepdims=True)
    p1, i1 = _first_index_of_max(p, idx)
    p2, i2 = _first_index_of_max(jnp.where(idx == i1, -1.0, p), idx)
    denom = p1 + p2
    e1 = gsel * EXPERTS_PER_GROUP + i1
    e2 = gsel * EXPERTS_PER_GROUP + i2
    rw_ref[...] = jnp.where(idx == 0, gw * p1 / denom, jnp.where(idx == 1, gw * p2 / denom, 0.0))

    ex = lax.broadcasted_iota(jnp.int32, (N_EXPERTS, n), 0)
    m1 = ex == e1
    m2 = ex == e2
    member = jnp.where(m1, 1.0, jnp.where(m2, 1.0, 0.0))
    before = (lax.broadcasted_iota(jnp.int32, (n, n), 0) < lax.broadcasted_iota(jnp.int32, (n, n), 1))
    rank = jnp.dot(member.astype(BF16), jnp.where(before, 1.0, 0.0).astype(BF16), preferred_element_type=F32)
    count = jnp.sum(member, axis=1, keepdims=True)
    chunks = jnp.floor((count + (MOE_CHUNK - 1)) * (1.0 / MOE_CHUNK))
    chunks_b = jnp.broadcast_to(chunks, (N_EXPERTS, LANES))
    lower = (lax.broadcasted_iota(jnp.int32, (N_EXPERTS, N_EXPERTS), 1)
             < lax.broadcasted_iota(jnp.int32, (N_EXPERTS, N_EXPERTS), 0))
    seg = jnp.dot(jnp.where(lower, 1.0, 0.0), chunks_b, precision=lax.Precision.HIGHEST,
                  preferred_element_type=F32)
    posmat = seg[:, 0:1] * MOE_CHUNK + rank
    pos1 = jnp.sum(jnp.where(m1, posmat, 0.0), axis=0, keepdims=True).astype(jnp.int32)
    pos2 = jnp.sum(jnp.where(m2, posmat, 0.0), axis=0, keepdims=True).astype(jnp.int32)
    ri_ref[...] = jnp.where(idx == 0, pos1, jnp.where(idx == 1, pos2, 0))
    pc_ref[...] = chunks_b
    slot = lax.broadcasted_iota(jnp.int32, (xs_ref.shape[0], n), 0)
    perm = jnp.where(slot == pos1, 1.0, jnp.where(slot == pos2, 1.0, 0.0)).astype(BF16)
    xs_ref[...] = jnp.dot(perm, h1.astype(BF16), preferred_element_type=F32)


def _merge(a, o2, rest, x2d, ln0g, ln0b, norm_g, wa, wb, wo, ln1g, ln1b, wr_t, br_col, *, heads, alpha):
    t, d = x2d.shape
    w = heads * LANES
    wa_w = a.shape[1]
    tm = MOE_TM
    n_tiles = t // tm
    full = lambda shape: pl.BlockSpec(shape, lambda i: (0,) * len(shape))
    return pl.pallas_call(
        functools.partial(_merge_kernel, heads=heads, alpha=alpha),
        grid=(t // tm,),
        in_specs=[
            pl.BlockSpec((tm, wa_w), lambda i: (i, 0)),
            pl.BlockSpec((2, tm, w), lambda i: (0, i, 0)),
            pl.BlockSpec((tm, w), lambda i: (i, 1)),
            pl.BlockSpec((tm, d), lambda i: (i, 1)),
            pl.BlockSpec((tm, d), lambda i: (i, 2)),
            pl.BlockSpec((tm, d), lambda i: (i, 0)),
            full((1, d)), full((1, d)), full((1, w)),
            full(wa.shape), full(wb.shape), full(wo.shape),
            full((1, d)), full((1, d)),
            full(wr_t.shape), full(br_col.shape),
        ],
        out_specs=[
            pl.BlockSpec((tm, d), lambda i: (i, 0)),
            pl.BlockSpec((EXPERTS_PER_GROUP, tm), lambda i: (0, i)),
            pl.BlockSpec((EXPERTS_PER_GROUP, tm), lambda i: (0, i)),
            pl.BlockSpec((MOE_CAP, d), lambda i: (i, 0)),
            pl.BlockSpec((N_EXPERTS, LANES), lambda i: (i, 0)),
        ],
        out_shape=[
            jax.ShapeDtypeStruct((t, d), F32),
            jax.ShapeDtypeStruct((EXPERTS_PER_GROUP, t), jnp.int32),
            jax.ShapeDtypeStruct((EXPERTS_PER_GROUP, t), F32),
            jax.ShapeDtypeStruct((n_tiles * MOE_CAP, d), F32),
            jax.ShapeDtypeStruct((n_tiles * N_EXPERTS, LANES), F32),
        ],
        compiler_params=pltpu.CompilerParams(
            dimension_semantics=("parallel",), vmem_limit_bytes=VMEM_LIMIT),
        name="merge_outproj_router",
    )(a, o2, rest, rest, rest, x2d, ln0g.reshape(1, d), ln0b.reshape(1, d), norm_g.reshape(1, w),
      wa, wb, wo, ln1g.reshape(1, d), ln1b.reshape(1, d), wr_t, br_col)


def _expert_kernel(be_ref, nv_ref, cur_ref, nxt_ref, xy_in, wg_ref, wu_ref, wd_ref, xy_out,
                   xbuf, ybuf, gsem, ssem):
    i = pl.program_id(0)
    n = pl.num_programs(0)
    slot = i % 2
    nv = nv_ref[i]

    def gather_copy(tbl_ref, s, buf_slot):
        src = pl.multiple_of(tbl_ref[0, 0, s] * MOE_CHUNK, MOE_CHUNK)
        dst = pl.multiple_of(s * MOE_CHUNK, MOE_CHUNK)
        return pltpu.make_async_copy(xy_in.at[pl.ds(src, MOE_CHUNK), :],
                                     xbuf.at[buf_slot, pl.ds(dst, MOE_CHUNK), :], gsem.at[buf_slot])

    def scatter_copy(s, buf_slot):
        src = pl.multiple_of(s * MOE_CHUNK, MOE_CHUNK)
        dst = pl.multiple_of(cur_ref[0, 0, s] * MOE_CHUNK, MOE_CHUNK)
        return pltpu.make_async_copy(ybuf.at[buf_slot, pl.ds(src, MOE_CHUNK), :],
                                     xy_out.at[pl.ds(dst, MOE_CHUNK), :], ssem.at[buf_slot])

    def start_gathers(tbl_ref, buf_slot):
        def body(s, c):
            gather_copy(tbl_ref, s, buf_slot).start()
            return c
        lax.fori_loop(0, MOE_CPB, body, 0)

    def wait_scatters(count, buf_slot):
        def body(s, c):
            scatter_copy(0, buf_slot).wait()
            return c
        lax.fori_loop(0, count, body, 0)

    @pl.when((i == 0) & (nv > 0))
    def _():
        start_gathers(cur_ref, slot)

    nxt = jnp.minimum(i + 1, n - 1)

    @pl.when((i + 1 < n) & (nv_ref[nxt] > 0))
    def _():
        start_gathers(nxt_ref, 1 - slot)

    @pl.when(i >= 2)
    def _():
        wait_scatters(nv_ref[jnp.maximum(i - 2, 0)], slot)

    @pl.when(nv > 0)
    def _():
        def g_wait(s, c):
            gather_copy(cur_ref, 0, slot).wait()
            return c
        lax.fori_loop(0, MOE_CPB, g_wait, 0)

        x = xbuf[slot].astype(BF16)
        gate = jnp.dot(x, wg_ref[0], preferred_element_type=F32)
        up = jnp.dot(x, wu_ref[0], preferred_element_type=F32)
        hid = (gate * _sigmoid(gate) * up).astype(BF16)
        ybuf[slot] = jnp.dot(hid, wd_ref[0], preferred_element_type=F32)

        def s_start(s, c):
            scatter_copy(s, slot).start()
            return c
        lax.fori_loop(0, nv, s_start, 0)

    @pl.when(i == n - 1)
    def _():
        @pl.when(i >= 1)
        def _():
            wait_scatters(nv_ref[jnp.maximum(i - 1, 0)], 1 - slot)
        wait_scatters(nv, slot)


def _experts(xy, block_expert, block_nvalid, table, wg, wu, wd):
    rows, d = xy.shape
    n_blocks = table.shape[0]
    ne, _, dff = wg.shape
    return pl.pallas_call(
        _expert_kernel,
        grid_spec=pltpu.PrefetchScalarGridSpec(
            num_scalar_prefetch=2,
            grid=(n_blocks,),
            in_specs=[
                pl.BlockSpec((1, 1, MOE_CPB), lambda i, be, nv: (i, 0, 0), memory_space=pltpu.SMEM),
                pl.BlockSpec((1, 1, MOE_CPB), lambda i, be, nv: (jnp.minimum(i + 1, n_blocks - 1), 0, 0),
                             memory_space=pltpu.SMEM),
                pl.BlockSpec(memory_space=pl.ANY),
                pl.BlockSpec((1, d, dff), lambda i, be, nv: (be[i], 0, 0)),
                pl.BlockSpec((1, d, dff), lambda i, be, nv: (be[i], 0, 0)),
                pl.BlockSpec((1, dff, d), lambda i, be, nv: (be[i], 0, 0)),
            ],
            out_specs=pl.BlockSpec(memory_space=pl.ANY),
            scratch_shapes=[
                pltpu.VMEM((2, MOE_BM, d), F32),
                pltpu.VMEM((2, MOE_BM, d), F32),
                pltpu.SemaphoreType.DMA((2,)),
                pltpu.SemaphoreType.DMA((2,)),
            ],
        ),
        out_shape=jax.ShapeDtypeStruct((rows, d), F32),
        input_output_aliases={4: 0},
        compiler_params=pltpu.CompilerParams(
            dimension_semantics=("arbitrary",), vmem_limit_bytes=VMEM_LIMIT),
        name="expert_ffn",
    )(block_expert, block_nvalid, table, table, xy, wg, wu, wd)


def _chunk_plan(tile_chunks):
    n_tiles = tile_chunks.shape[0]
    cap_chunks = MOE_CAP // MOE_CHUNK
    max_tile_chunks = (TOP_K * MOE_TM + N_EXPERTS * (MOE_CHUNK - 1)) // MOE_CHUNK
    n_blocks = -(-n_tiles * max_tile_chunks // MOE_CPB) + N_EXPERTS
    pc = tile_chunks.astype(jnp.int32)
    seg = jnp.cumsum(pc, axis=1) - pc
    tile_end = jnp.cumsum(pc.T, axis=1)
    tile_beg = tile_end - pc.T
    tot = tile_end[:, -1]
    nblk = (tot + MOE_CPB - 1) // MOE_CPB
    bend = jnp.cumsum(nblk)
    bbeg = bend - nblk
    bidx = jnp.arange(n_blocks, dtype=jnp.int32)
    be = jnp.minimum(jnp.sum((bend[None, :] <= bidx[:, None]).astype(jnp.int32), axis=1), N_EXPERTS - 1)
    onehot_e = (be[:, None] == jnp.arange(N_EXPERTS, dtype=jnp.int32)[None, :]).astype(jnp.int32)
    pick = lambda vec: jnp.sum(onehot_e * vec[None, :], axis=1)
    q = ((bidx - pick(bbeg)) * MOE_CPB)[:, None] + jnp.arange(MOE_CPB, dtype=jnp.int32)[None, :]
    valid = (q < pick(tot)[:, None]) & (bidx < bend[-1])[:, None]
    rows_of = lambda tbl: jnp.sum(onehot_e[:, :, None] * tbl[None, :, :], axis=1)
    blk_end = rows_of(tile_end)
    base = jnp.arange(n_tiles, dtype=jnp.int32)[None, :] * cap_chunks + seg.T - tile_beg
    blk_base = rows_of(base)
    tile = jnp.sum((blk_end[:, None, :] <= q[:, :, None]).astype(jnp.int32), axis=2)
    tile = jnp.minimum(tile, n_tiles - 1)
    onehot_t = tile[:, :, None] == jnp.arange(n_tiles, dtype=jnp.int32)[None, None, :]
    src = q + jnp.sum(jnp.where(onehot_t, blk_base[:, None, :], 0), axis=2)
    src = jnp.where(valid, src, src[:, 0:1])
    nvalid = jnp.sum(valid, axis=1).astype(jnp.int32)
    src = jnp.where(nvalid[:, None] > 0, src, 0).astype(jnp.int32)
    return be.astype(jnp.int32), nvalid, src.reshape(n_blocks, 1, MOE_CPB)


def _final_kernel(h1_ref, ys_ref, pos_ref, w_ref, g_ref, b_ref, o_ref, *, alpha):
    cap = ys_ref.shape[0]
    n = h1_ref.shape[0]
    slot = lax.broadcasted_iota(jnp.int32, (cap, n), 0)
    pw = (jnp.where(slot == pos_ref[0:1, :], w_ref[0:1, :], 0.0)
          + jnp.where(slot == pos_ref[1:2, :], w_ref[1:2, :], 0.0))
    pw_hi = pw.astype(BF16)
    pw_lo = (pw - pw_hi.astype(F32)).astype(BF16)
    ys = ys_ref[...].astype(BF16)
    tn = (((0,), (0,)), ((), ()))
    moe = (lax.dot_general(pw_hi, ys, tn, preferred_element_type=F32)
           + lax.dot_general(pw_lo, ys, tn, preferred_element_type=F32))
    o_ref[...] = _ln(alpha * h1_ref[...] + moe, g_ref[...], b_ref[...])


def _final(h1, ys, pos, wts, g, b, *, alpha):
    t, d = h1.shape
    tm = MOE_TM
    return pl.pallas_call(
        functools.partial(_final_kernel, alpha=alpha),
        grid=(t // tm,),
        in_specs=[
            pl.BlockSpec((tm, d), lambda i: (i, 0)),
            pl.BlockSpec((MOE_CAP, d), lambda i: (i, 0)),
            pl.BlockSpec((EXPERTS_PER_GROUP, tm), lambda i: (0, i)),
            pl.BlockSpec((EXPERTS_PER_GROUP, tm), lambda i: (0, i)),
            pl.BlockSpec((1, d), lambda i: (0, 0)),
            pl.BlockSpec((1, d), lambda i: (0, 0)),
        ],
        out_specs=pl.BlockSpec((tm, d), lambda i: (i, 0)),
        out_shape=jax.ShapeDtypeStruct((t, d), F32),
        compiler_params=pltpu.CompilerParams(
            dimension_semantics=("parallel",), vmem_limit_bytes=VMEM_LIMIT),
        name="combine_postnorm",
    )(h1, ys, pos, wts, g.reshape(1, d), b.reshape(1, d))


def kernel(x, emb_ln_g, emb_ln_b, w_in, na_rpb, hg_lb, hg_norm_g, w_proj_a, w_proj_b, w_out, ln1_g, ln1_b,
           w_router_group, b_router_group, w_router_expert, b_router_expert, w_gate, w_up, w_down, ln2_g, ln2_b):
    batch, seq, d = x.shape
    depth = w_in.shape[0]
    t = batch * seq
    na_w = na_rpb.shape[1] * NA_HEAD_DIM
    hg_w = hg_norm_g.shape[1]
    hg_heads = hg_w // HG_DK
    alpha = float((2 * depth) ** 0.25)
    lb_all = jnp.cumsum(jax.nn.softmax(hg_lb.astype(F32), axis=1), axis=1)

    assert depth == 1, depth
    l = 0
    x2d = x.reshape(t, d)
    na, hg, rest = _inproj(x2d, emb_ln_g, emb_ln_b, w_in[l].astype(BF16), n_na=3 * na_w, n_hg=3 * hg_w,
                           q_cols=na_w, q_scale=float(NA_HEAD_DIM ** -0.5))
    bias = _na_bias_tables(na_rpb[l], seq // GRID_W)
    a = _na(na.reshape(batch, seq, 3 * na_w), bias, batch=batch, seq=seq).reshape(t, na_w)
    o2 = _hgrn(hg, rest, lb_all[:, l], batch=batch, seq=seq, heads=hg_heads)

    wr_t = jnp.zeros((LANES, d), F32)
    wr_t = wr_t.at[0:N_GROUPS].set(w_router_group[l].T.astype(F32))
    wr_t = wr_t.at[EXPERTS_PER_GROUP:EXPERTS_PER_GROUP + N_EXPERTS].set(w_router_expert[l].T.astype(F32))
    br = jnp.zeros((LANES,), F32)
    br = br.at[0:N_GROUPS].set(b_router_group[l].astype(F32))
    br = br.at[EXPERTS_PER_GROUP:EXPERTS_PER_GROUP + N_EXPERTS].set(b_router_expert[l].astype(F32))
    br_col = jnp.broadcast_to(br[:, None], (LANES, LANES))

    h1, pos, route_w, xs, tile_chunks = _merge(
        a, o2, rest, x2d, emb_ln_g, emb_ln_b, hg_norm_g[l], w_proj_a[l].astype(BF16), w_proj_b[l].astype(BF16),
        w_out[l].astype(BF16), ln1_g[l], ln1_b[l], wr_t, br_col, heads=hg_heads, alpha=alpha)

    plan = _chunk_plan(tile_chunks[:, 0].reshape(t // MOE_TM, N_EXPERTS))
    ys = _experts(xs, *plan, w_gate[l].astype(BF16), w_up[l].astype(BF16), w_down[l].astype(BF16))
    h2 = _final(h1, ys, pos, route_w, ln2_g[l], ln2_b[l], alpha=alpha)
    return h2.reshape(batch, seq, d)
```

```python
import functools

import numpy as np
import jax
import jax.numpy as jnp
from jax import lax
from jax.experimental import pallas as pl
from jax.experimental.pallas import tpu as pltpu

F32 = jnp.float32
BF16 = jnp.bfloat16

GRID_W = 64
NA_HEAD_DIM = 64
NA_KH = 8
NA_KW = 16
HG_DK = 128
N_GROUPS = 4
EXPERTS_PER_GROUP = 8
N_EXPERTS = N_GROUPS * EXPERTS_PER_GROUP
TOP_K = 2
LN_EPS = 1e-5
RMS_EPS = 1e-6

LANES = 128
SUBLANES = 8
VMEM_LIMIT = 56 * 1024 * 1024
NEG_BIG = -1e30
LOG2_E = 1.4426950408889634

NA_QROWS = 4
NA_KROWS = NA_QROWS + NA_KH
HG_CHUNK = 128
HG_LEVELS = 7
MOE_TM = 256
MOE_CHUNK = 8
MOE_CAP = TOP_K * MOE_TM + N_EXPERTS * MOE_CHUNK
MOE_BM = 256
MOE_CPB = MOE_BM // MOE_CHUNK
MOE_DMA_GROUP = 8


def _ln(x, g, b):
    mu = jnp.mean(x, axis=-1, keepdims=True)
    xc = x - mu
    var = jnp.mean(xc * xc, axis=-1, keepdims=True)
    return xc * lax.rsqrt(var + LN_EPS) * g + b


def _sigmoid(x):
    return 1.0 / (1.0 + jnp.exp(-x))


def _inproj_kernel(x_ref, g_ref, b_ref, w_ref, lb_ref, na_ref, hg_ref, rest_ref, *, n_na, hg_w, q_cols, q_scale):
    h = _ln(x_ref[...], g_ref[...], b_ref[...]).astype(BF16)
    d_in = w_ref.shape[1]
    cn = hg_w
    for c0 in range(0, d_in, cn):
        u = jnp.dot(h, w_ref[:, c0:c0 + cn], preferred_element_type=F32)
        if c0 < q_cols:
            u = u * q_scale
        if c0 < n_na:
            na_ref[:, c0:c0 + cn] = u.astype(na_ref.dtype)
        elif c0 == n_na:
            hg_ref[:, 0:cn] = (u * _sigmoid(u)).astype(hg_ref.dtype)
        elif c0 < n_na + 3 * hg_w:
            d = (c0 - n_na) // hg_w - 1
            lb = lb_ref[d]
            forget = lb + (1.0 - lb) * _sigmoid(u)
            logf = jnp.log(forget)
            hi = logf.astype(BF16)
            hg_ref[:, (1 + d) * cn:(2 + d) * cn] = (1.0 - forget).astype(hg_ref.dtype)
            hg_ref[:, (3 + 2 * d) * cn:(4 + 2 * d) * cn] = hi
            hg_ref[:, (4 + 2 * d) * cn:(5 + 2 * d) * cn] = (logf - hi.astype(F32)).astype(BF16)
        else:
            o = c0 - n_na - 3 * hg_w
            rest_ref[:, o:o + cn] = u.astype(rest_ref.dtype)


def _inproj(x2d, g, b, w_bf16, lb, *, n_na, hg_w, q_cols, q_scale, tm=512):
    t, d = x2d.shape
    d_in = w_bf16.shape[1]
    n_rest = d_in - n_na - 3 * hg_w
    kern = functools.partial(_inproj_kernel, n_na=n_na, hg_w=hg_w, q_cols=q_cols, q_scale=q_scale)
    return pl.pallas_call(
        kern,
        grid=(t // tm,),
        in_specs=[
            pl.BlockSpec((tm, d), lambda i: (i, 0)),
            pl.BlockSpec((1, d), lambda i: (0, 0)),
            pl.BlockSpec((1, d), lambda i: (0, 0)),
            pl.BlockSpec((d, d_in), lambda i: (0, 0), pipeline_mode=pl.Buffered(1)),
            pl.BlockSpec((2, 1, hg_w), lambda i: (0, 0, 0)),
        ],
        out_specs=[
            pl.BlockSpec((tm, n_na), lambda i: (i, 0)),
            pl.BlockSpec((tm, 7 * hg_w), lambda i: (i, 0)),
            pl.BlockSpec((tm, n_rest), lambda i: (i, 0)),
        ],
        out_shape=[
            jax.ShapeDtypeStruct((t, n_na), BF16),
            jax.ShapeDtypeStruct((t, 7 * hg_w), BF16),
            jax.ShapeDtypeStruct((t, n_rest), BF16),
        ],
        compiler_params=pltpu.CompilerParams(
            dimension_semantics=("parallel",), vmem_limit_bytes=VMEM_LIMIT),
        name="ln_inproj",
    )(x2d, g.reshape(1, d), b.reshape(1, d), w_bf16, lb.reshape(2, 1, hg_w))


def _na_bias_tables(rpb, rows):
    nq = NA_QROWS * GRID_W
    nk = NA_KROWS * GRID_W
    qi = np.arange(nq)
    kj = np.arange(nk)
    qc = (qi % GRID_W)[:, None]
    kc = (kj % GRID_W)[None, :]
    cs = np.clip(qc - NA_KW // 2, 0, GRID_W - NA_KW)
    valid_c = (kc >= cs) & (kc < cs + NA_KW)
    n_rel_r, n_rel_c = 2 * NA_KH - 1, 2 * NA_KW - 1
    wc = np.arange(GRID_W)
    cidx = np.clip(wc[None, :] - wc[:, None] + NA_KW - 1, 0, n_rel_c - 1)
    onehot_c = (cidx[None] == np.arange(n_rel_c)[:, None, None]).astype(np.float32)
    ridx_all, valid_all = [], []
    for r0 in (0, NA_QROWS, rows - NA_QROWS):
        start = int(np.clip(r0 - NA_KH // 2, 0, rows - NA_KROWS))
        qr = (r0 + qi // GRID_W)[:, None]
        kr = (start + kj // GRID_W)[None, :]
        rs = np.clip(qr - NA_KH // 2, 0, rows - NA_KH)
        valid_all.append((kr >= rs) & (kr < rs + NA_KH) & valid_c)
        qr_s = r0 + np.arange(NA_QROWS)[:, None]
        kr_s = start + np.arange(NA_KROWS)[None, :]
        ridx_all.append(np.clip(kr_s - qr_s + NA_KH - 1, 0, n_rel_r - 1))
    valid = np.stack(valid_all)
    rows_sel = rpb.astype(F32)[:, np.stack(ridx_all), :]
    bias = jnp.einsum('hcqkj,jxy->chqxky', rows_sel, onehot_c, precision=lax.Precision.HIGHEST)
    bias = bias.reshape(3, rpb.shape[0], nq, nk) * LOG2_E
    return jnp.where(valid[:, None], bias, NEG_BIG)


def _na_kernel(q_ref, k_ref, v_ref, bias_ref, o_ref, *, rows):
    rb = pl.program_id(2)
    start = jnp.clip(rb * NA_QROWS - NA_KH // 2, 0, rows - NA_KROWS)
    kstart = pl.multiple_of(start * GRID_W, NA_QROWS * GRID_W)
    nk = NA_KROWS * GRID_W
    kwin = k_ref[0, pl.ds(kstart, nk), :]
    vwin = v_ref[0, pl.ds(kstart, nk), :]
    q = q_ref[0]
    lane = lax.broadcasted_iota(jnp.int32, (1, LANES), 1)
    first = lane < NA_HEAD_DIM
    zero = jnp.zeros_like(q)
    qs = [jnp.where(first, q, zero), jnp.where(first, zero, q)]
    ss = [lax.dot_general(qh, kwin, (((1,), (1,)), ((), ())), preferred_element_type=F32) + bias_ref[0, 0, hh]
          for hh, qh in enumerate(qs)]
    ps = [jnp.exp2(s - jnp.max(s, axis=-1, keepdims=True)) for s in ss]
    ls = [jnp.sum(p, axis=-1, keepdims=True) for p in ps]
    os = [jnp.dot(p.astype(BF16), vwin, preferred_element_type=F32) for p in ps]
    o_ref[0] = jnp.where(first, os[0] / ls[0], os[1] / ls[1]).astype(o_ref.dtype)


def _na(qkv, bias, *, batch, seq):
    rows = seq // GRID_W
    width = qkv.shape[-1] // 3
    pairs = width // LANES
    nq = NA_QROWS * GRID_W
    nk = NA_KROWS * GRID_W
    nrb = rows // NA_QROWS
    bias5 = bias.reshape(3, pairs, 2, nq, nk)

    def bias_map(b, p, r):
        cls = jnp.where(r == 0, 0, jnp.where(r == nrb - 1, 2, 1))
        return (cls, p, 0, 0, 0)

    return pl.pallas_call(
        functools.partial(_na_kernel, rows=rows),
        grid=(batch, pairs, nrb),
        in_specs=[
            pl.BlockSpec((1, nq, LANES), lambda b, p, r: (b, r, p)),
            pl.BlockSpec((1, seq, LANES), lambda b, p, r: (b, 0, pairs + p)),
            pl.BlockSpec((1, seq, LANES), lambda b, p, r: (b, 0, 2 * pairs + p)),
            pl.BlockSpec((1, 1, 2, nq, nk), bias_map),
        ],
        out_specs=pl.BlockSpec((1, nq, LANES), lambda b, p, r: (b, r, p)),
        out_shape=jax.ShapeDtypeStruct((batch, seq, width), BF16),
        compiler_params=pltpu.CompilerParams(
            dimension_semantics=("parallel", "parallel", "arbitrary"), vmem_limit_bytes=VMEM_LIMIT),
        name="neighbourhood_attention",
    )(qkv, qkv, qkv, bias5)


def _hg_tables():
    c, nl = HG_CHUNK, HG_LEVELS
    a = np.zeros((2, c * (nl + 1) + 8, c), np.float32)
    lvl = np.full((2, c, c), -1, np.int32)
    for rev in (0, 1):
        for t in range(c):
            if rev == 0:
                a[rev, t, :t + 1] = 1.0
            else:
                a[rev, t, t:] = 1.0
            lvl[rev, t, t] = nl
            for s in range(c):
                if (s < t and rev == 0) or (s > t and rev == 1):
                    lvl[rev, t, s] = int(np.floor(np.log2(t ^ s)))
            for l in range(nl):
                m = 1 << l
                mid = (t & ~(2 * m - 1)) + m
                upper = (t >> l) & 1
                row = c * (l + 1) + t
                if rev == 0:
                    lo, hi = (mid, t) if upper else (t + 1, mid - 1)
                else:
                    lo, hi = (mid, t - 1) if upper else (t, mid - 1)
                if hi >= lo:
                    a[rev, row, lo:hi + 1] = 1.0
        a[rev, c * (nl + 1):, :] = 1.0
    return a, lvl


def _hg_kernel(q_ref, k_ref, up_ref, dn_ref, ghi_ref, glo_ref, i_ref, a_ref, lvl_ref, o_ref, st_ref, *,
               heads, n_chunks):
    rev = pl.program_id(1)
    c, nl = HG_CHUNK, HG_LEVELS

    @pl.when(pl.program_id(2) == 0)
    def _():
        st_ref[...] = jnp.zeros_like(st_ref)

    a_mat = a_ref[0]
    lvl = lvl_ref[0]
    w = heads * LANES
    sub = lax.broadcasted_iota(jnp.int32, (SUBLANES, w), 0)
    nt = (((1,), (1,)), ((), ()))
    tn = (((0,), (0,)), ((), ()))
    head_cols = [slice(h * LANES, (h + 1) * LANES) for h in range(heads)]

    def level_operand(l, up, dn):
        m = 1 << l
        if m >= SUBLANES:
            return jnp.concatenate([(up if k & 1 else dn)[k * m:(k + 1) * m] for k in range(c // m)], axis=0)
        upper = ((sub >> l) & 1) == 1
        shape3 = (c // SUBLANES, SUBLANES, w)
        return jnp.where(upper[None], up.reshape(shape3), dn.reshape(shape3)).reshape(c, w)

    def chunk(j, carry):
        cj = jnp.where(rev == 1, n_chunks - 1 - j, j)
        r0 = pl.multiple_of(cj * c, c)
        rows = pl.ds(r0, c)
        qb = q_ref[rows, :]
        kb = k_ref[rows, :]
        q = qb.astype(F32)
        kk = kb.astype(F32)
        up = up_ref[rows, :].astype(F32)
        dn = dn_ref[rows, :].astype(F32)
        v = i_ref[rows, :]
        e = jnp.dot(a_mat, jnp.concatenate([ghi_ref[rows, :], glo_ref[rows, :]], axis=0),
                    preferred_element_type=F32)
        b = e[0:c]
        btot = e[c * (nl + 1):c * (nl + 1) + 1]
        qd = (q * jnp.exp(b)).astype(BF16)
        khat = (kk * jnp.exp(btot - b)).astype(BF16)
        decay = jnp.exp(btot)
        xs = [(level_operand(l, up, dn) * jnp.exp(e[c * (l + 1):c * (l + 2)])).astype(BF16) for l in range(nl)]
        sts = [st_ref[h] for h in range(heads)]
        outs = [lax.dot_general(qd[:, hc], sts[h].astype(BF16), nt, preferred_element_type=F32)
                for h, hc in enumerate(head_cols)]
        attns = [jnp.where(lvl == nl, lax.dot_general(qb[:, hc], kb[:, hc], nt, preferred_element_type=F32), 0.0)
                 for hc in head_cols]
        for l in range(nl):
            x = xs[l]
            attns = [jnp.where(lvl == l, lax.dot_general(x[:, hc], x[:, hc], nt, preferred_element_type=F32), at)
                     for hc, at in zip(head_cols, attns)]
        for h, hc in enumerate(head_cols):
            o_ref[0, rows, hc] = outs[h] + jnp.dot(attns[h].astype(BF16), v[:, hc],
                                                   preferred_element_type=F32)
        for h, hc in enumerate(head_cols):
            st_ref[h] = sts[h] * decay[:, hc] + lax.dot_general(v[:, hc], khat[:, hc], tn,
                                                                 preferred_element_type=F32)
        return carry

    lax.fori_loop(0, n_chunks, chunk, 0)


def _hgrn(hg, rest, *, batch, seq, heads, cb=512):
    t = batch * seq
    w = heads * LANES
    nsb = seq // cb
    a_np, lvl_np = _hg_tables()
    a_np = np.concatenate([a_np, a_np], axis=-1)
    a_mat = jnp.asarray(a_np, BF16)
    lvl = jnp.asarray(lvl_np)

    def rowblk(b, d, s):
        return b * nsb + jnp.where(d == 1, nsb - 1 - s, s)

    def col(fn):
        return pl.BlockSpec((cb, w), lambda b, d, s: (rowblk(b, d, s), fn(d)))

    return pl.pallas_call(
        functools.partial(_hg_kernel, heads=heads, n_chunks=cb // HG_CHUNK),
        grid=(batch, 2, nsb),
        in_specs=[
            col(lambda d: 0),
            col(lambda d: 1 + d),
            col(lambda d: 2 * d),
            col(lambda d: 1 - d),
            col(lambda d: 3 + 2 * d),
            col(lambda d: 4 + 2 * d),
            col(lambda d: 0),
            pl.BlockSpec((1,) + a_np.shape[1:], lambda b, d, s: (d, 0, 0)),
            pl.BlockSpec((1,) + lvl_np.shape[1:], lambda b, d, s: (d, 0, 0)),
        ],
        out_specs=pl.BlockSpec((1, cb, w), lambda b, d, s: (d, rowblk(b, d, s), 0)),
        out_shape=jax.ShapeDtypeStruct((2, t, w), F32),
        scratch_shapes=[pltpu.VMEM((heads, LANES, LANES), F32)],
        compiler_params=pltpu.CompilerParams(
            dimension_semantics=("parallel", "parallel", "arbitrary"), vmem_limit_bytes=VMEM_LIMIT),
        name="hgrn2_scan",
    )(hg, hg, hg, hg, hg, hg, rest, a_mat, lvl)


def _first_index_of_max(x, idx):
    m = jnp.max(x, axis=0, keepdims=True)
    i = jnp.min(jnp.where(x == m, idx, EXPERTS_PER_GROUP), axis=0, keepdims=True)
    return m, i


def _merge_kernel(a_ref, o2_ref, g_ref, ga_ref, gb_ref, x_ref, ln0g_ref, ln0b_ref, ng_ref,
                  wa_ref, wb_ref, wo_ref, ln1g_ref, ln1b_ref, wrh_ref, wrl_ref, br_ref,
                  h1_ref, ri_ref, rw_ref, xs_ref, pc_ref, *, heads, alpha):
    o = o2_ref[0] + o2_ref[1]
    gate = g_ref[...].astype(F32)
    parts = []
    for h in range(heads):
        cols = slice(h * LANES, (h + 1) * LANES)
        oh = o[:, cols]
        ms = jnp.mean(oh * oh, axis=-1, keepdims=True)
        gh = gate[:, cols]
        parts.append((oh * lax.rsqrt(ms + RMS_EPS) * ng_ref[:, cols] * (gh * _sigmoid(gh))).astype(BF16))
    c = jnp.concatenate(parts, axis=-1)
    pa = jnp.dot(a_ref[...], wa_ref[...], preferred_element_type=F32)
    pb = jnp.dot(c, wb_ref[...], preferred_element_type=F32)
    merged = _sigmoid(ga_ref[...].astype(F32)) * pa + _sigmoid(gb_ref[...].astype(F32)) * pb
    y = jnp.dot(merged.astype(BF16), wo_ref[...], preferred_element_type=F32)
    h0 = _ln(x_ref[...], ln0g_ref[...], ln0b_ref[...])
    h1 = _ln(alpha * h0 + y, ln1g_ref[...], ln1b_ref[...])
    h1_ref[...] = h1

    nt = (((1,), (1,)), ((), ()))
    h_hi = h1.astype(BF16)
    h_lo = (h1 - h_hi.astype(F32)).astype(BF16)
    logits = (lax.dot_general(wrh_ref[...], h_hi, nt, preferred_element_type=F32)
              + lax.dot_general(wrh_ref[...], h_lo, nt, preferred_element_type=F32)
              + lax.dot_general(wrl_ref[...], h_hi, nt, preferred_element_type=F32) + br_ref[:, 0:1])
    n = logits.shape[1]
    idx = lax.broadcasted_iota(jnp.int32, (EXPERTS_PER_GROUP, n), 0)
    glog = jnp.where(idx < N_GROUPS, logits[0:EXPERTS_PER_GROUP], -jnp.inf)
    gmax, gsel = _first_index_of_max(glog, idx)
    gw = 1.0 / jnp.sum(jnp.exp(glog - gmax), axis=0, keepdims=True)
    elog = logits[EXPERTS_PER_GROUP:2 * EXPERTS_PER_GROUP]
    for gi in range(1, N_GROUPS):
        lo = EXPERTS_PER_GROUP * (gi + 1)
        elog = jnp.where(gsel == gi, logits[lo:lo + EXPERTS_PER_GROUP], elog)
    ee = jnp.exp(elog - jnp.max(elog, axis=0, keepdims=True))
    p = ee / jnp.sum(ee, axis=0, keepdims=True)
    p1, i1 = _first_index_of_max(p, idx)
    p2, i2 = _first_index_of_max(jnp.where(idx == i1, -1.0, p), idx)
    denom = p1 + p2
    e1 = gsel * EXPERTS_PER_GROUP + i1
    e2 = gsel * EXPERTS_PER_GROUP + i2
    rw_ref[...] = jnp.where(idx == 0, gw * p1 / denom, jnp.where(idx == 1, gw * p2 / denom, 0.0))

    ex = lax.broadcasted_iota(jnp.int32, (N_EXPERTS, n), 0)
    m1 = ex == e1
    m2 = ex == e2
    member = jnp.where(m1, 1.0, jnp.where(m2, 1.0, 0.0))
    before = (lax.broadcasted_iota(jnp.int32, (n, n), 0) < lax.broadcasted_iota(jnp.int32, (n, n), 1))
    rank = jnp.dot(member.astype(BF16), jnp.where(before, 1.0, 0.0).astype(BF16), preferred_element_type=F32)
    count = jnp.sum(member, axis=1, keepdims=True)
    chunks = jnp.floor((count + (MOE_CHUNK - 1)) * (1.0 / MOE_CHUNK))
    chunks_b = jnp.broadcast_to(chunks, (N_EXPERTS, LANES))
    lower = (lax.broadcasted_iota(jnp.int32, (N_EXPERTS, N_EXPERTS), 1)
             < lax.broadcasted_iota(jnp.int32, (N_EXPERTS, N_EXPERTS), 0))
    seg = jnp.dot(jnp.where(lower, 1.0, 0.0), chunks_b, precision=lax.Precision.HIGHEST,
                  preferred_element_type=F32)
    posmat = seg[:, 0:1] * MOE_CHUNK + rank
    pos1 = jnp.sum(jnp.where(m1, posmat, 0.0), axis=0, keepdims=True).astype(jnp.int32)
    pos2 = jnp.sum(jnp.where(m2, posmat, 0.0), axis=0, keepdims=True).astype(jnp.int32)
    ri_ref[...] = jnp.where(idx == 0, pos1, jnp.where(idx == 1, pos2, 0))
    pc_ref[...] = chunks_b
    slot = lax.broadcasted_iota(jnp.int32, (xs_ref.shape[0], n), 0)
    perm = jnp.where(slot == pos1, 1.0, jnp.where(slot == pos2, 1.0, 0.0)).astype(BF16)
    xs_ref[...] = jnp.dot(perm, h_hi, preferred_element_type=F32)


def _merge(a, o2, rest, x2d, ln0g, ln0b, norm_g, wa, wb, wo, ln1g, ln1b, wr_t, br_col, *, heads, alpha):
    t, d = x2d.shape
    w = heads * LANES
    wa_w = a.shape[1]
    tm = MOE_TM
    n_tiles = t // tm
    wr_hi = wr_t.astype(BF16)
    wr_lo = (wr_t - wr_hi.astype(F32)).astype(BF16)
    full = lambda shape: pl.BlockSpec(shape, lambda i: (0,) * len(shape))
    return pl.pallas_call(
        functools.partial(_merge_kernel, heads=heads, alpha=alpha),
        grid=(t // tm,),
        in_specs=[
            pl.BlockSpec((tm, wa_w), lambda i: (i, 0)),
            pl.BlockSpec((2, tm, w), lambda i: (0, i, 0)),
            pl.BlockSpec((tm, w), lambda i: (i, 1)),
            pl.BlockSpec((tm, d), lambda i: (i, 1)),
            pl.BlockSpec((tm, d), lambda i: (i, 2)),
            pl.BlockSpec((tm, d), lambda i: (i, 0)),
            full((1, d)), full((1, d)), full((1, w)),
            full(wa.shape), full(wb.shape), full(wo.shape),
            full((1, d)), full((1, d)),
            full(wr_t.shape), full(wr_t.shape), full(br_col.shape),
        ],
        out_specs=[
            pl.BlockSpec((tm, d), lambda i: (i, 0)),
            pl.BlockSpec((EXPERTS_PER_GROUP, tm), lambda i: (0, i)),
            pl.BlockSpec((EXPERTS_PER_GROUP, tm), lambda i: (0, i)),
            pl.BlockSpec((MOE_CAP, d), lambda i: (i, 0)),
            pl.BlockSpec((N_EXPERTS, LANES), lambda i: (i, 0)),
        ],
        out_shape=[
            jax.ShapeDtypeStruct((t, d), F32),
            jax.ShapeDtypeStruct((EXPERTS_PER_GROUP, t), jnp.int32),
            jax.ShapeDtypeStruct((EXPERTS_PER_GROUP, t), F32),
            jax.ShapeDtypeStruct((n_tiles * MOE_CAP, d), F32),
            jax.ShapeDtypeStruct((n_tiles * N_EXPERTS, LANES), F32),
        ],
        compiler_params=pltpu.CompilerParams(
            dimension_semantics=("parallel",), vmem_limit_bytes=VMEM_LIMIT),
        name="merge_outproj_router",
    )(a, o2, rest, rest, rest, x2d, ln0g.reshape(1, d), ln0b.reshape(1, d), norm_g.reshape(1, w),
      wa, wb, wo, ln1g.reshape(1, d), ln1b.reshape(1, d), wr_hi, wr_lo, br_col)


def _expert_kernel(be_ref, nv_ref, cur_ref, nxt_ref, xy_in, wg_ref, wu_ref, wd_ref, xy_out,
                   xbuf, ybuf, gsem, ssem):
    i = pl.program_id(0)
    n = pl.num_programs(0)
    slot = i % 2
    nv = nv_ref[i]

    def gather_copy(tbl_ref, s, buf_slot):
        src = pl.multiple_of(tbl_ref[0, 0, s] * MOE_CHUNK, MOE_CHUNK)
        dst = pl.multiple_of(s * MOE_CHUNK, MOE_CHUNK)
        return pltpu.make_async_copy(xy_in.at[pl.ds(src, MOE_CHUNK), :],
                                     xbuf.at[buf_slot, pl.ds(dst, MOE_CHUNK), :], gsem.at[buf_slot])

    def scatter_copy(s, buf_slot):
        src = pl.multiple_of(s * MOE_CHUNK, MOE_CHUNK)
        dst = pl.multiple_of(cur_ref[0, 0, s] * MOE_CHUNK, MOE_CHUNK)
        return pltpu.make_async_copy(ybuf.at[buf_slot, pl.ds(src, MOE_CHUNK), :],
                                     xy_out.at[pl.ds(dst, MOE_CHUNK), :], ssem.at[buf_slot])

    def for_each_chunk(count, fn):
        for g0 in range(0, MOE_CPB, MOE_DMA_GROUP):
            @pl.when(count >= g0 + MOE_DMA_GROUP)
            def _():
                for s in range(g0, g0 + MOE_DMA_GROUP):
                    fn(s)

        def body(s, c):
            fn(s)
            return c
        lax.fori_loop(count // MOE_DMA_GROUP * MOE_DMA_GROUP, count, body, 0)

    def start_gathers(tbl_ref, buf_slot):
        for s in range(MOE_CPB):
            gather_copy(tbl_ref, s, buf_slot).start()

    def wait_scatters(count, buf_slot):
        for_each_chunk(count, lambda s: scatter_copy(0, buf_slot).wait())

    @pl.when((i == 0) & (nv > 0))
    def _():
        start_gathers(cur_ref, slot)

    nxt = jnp.minimum(i + 1, n - 1)

    @pl.when((i + 1 < n) & (nv_ref[nxt] > 0))
    def _():
        start_gathers(nxt_ref, 1 - slot)

    @pl.when(i >= 2)
    def _():
        wait_scatters(nv_ref[jnp.maximum(i - 2, 0)], slot)

    @pl.when(nv > 0)
    def _():
        for _ in range(MOE_CPB):
            gather_copy(cur_ref, 0, slot).wait()

        x = xbuf[slot].astype(BF16)
        gate = jnp.dot(x, wg_ref[0], preferred_element_type=F32)
        up = jnp.dot(x, wu_ref[0], preferred_element_type=F32)
        hid = (gate * _sigmoid(gate) * up).astype(BF16)
        ybuf[slot] = jnp.dot(hid, wd_ref[0], preferred_element_type=F32)

        for_each_chunk(nv, lambda s: scatter_copy(s, slot).start())

    @pl.when(i == n - 1)
    def _():
        @pl.when(i >= 1)
        def _():
            wait_scatters(nv_ref[jnp.maximum(i - 1, 0)], 1 - slot)
        wait_scatters(nv, slot)


def _experts(xy, block_expert, block_nvalid, table, wg, wu, wd):
    rows, d = xy.shape
    n_blocks = table.shape[0]
    ne, _, dff = wg.shape
    return pl.pallas_call(
        _expert_kernel,
        grid_spec=pltpu.PrefetchScalarGridSpec(
            num_scalar_prefetch=2,
            grid=(n_blocks,),
            in_specs=[
                pl.BlockSpec((1, 1, MOE_CPB), lambda i, be, nv: (i, 0, 0), memory_space=pltpu.SMEM),
                pl.BlockSpec((1, 1, MOE_CPB), lambda i, be, nv: (jnp.minimum(i + 1, n_blocks - 1), 0, 0),
                             memory_space=pltpu.SMEM),
                pl.BlockSpec(memory_space=pl.ANY),
                pl.BlockSpec((1, d, dff), lambda i, be, nv: (be[i], 0, 0)),
                pl.BlockSpec((1, d, dff), lambda i, be, nv: (be[i], 0, 0)),
                pl.BlockSpec((1, dff, d), lambda i, be, nv: (be[i], 0, 0)),
            ],
            out_specs=pl.BlockSpec(memory_space=pl.ANY),
            scratch_shapes=[
                pltpu.VMEM((2, MOE_BM, d), F32),
                pltpu.VMEM((2, MOE_BM, d), F32),
                pltpu.SemaphoreType.DMA((2,)),
                pltpu.SemaphoreType.DMA((2,)),
            ],
        ),
        out_shape=jax.ShapeDtypeStruct((rows, d), F32),
        input_output_aliases={4: 0},
        compiler_params=pltpu.CompilerParams(
            dimension_semantics=("arbitrary",), vmem_limit_bytes=VMEM_LIMIT),
        name="expert_ffn",
    )(block_expert, block_nvalid, table, table, xy, wg, wu, wd)


def _chunk_plan(tile_chunks):
    n_tiles = tile_chunks.shape[0]
    cap_chunks = MOE_CAP // MOE_CHUNK
    max_tile_chunks = (TOP_K * MOE_TM + N_EXPERTS * (MOE_CHUNK - 1)) // MOE_CHUNK
    n_blocks = -(-n_tiles * max_tile_chunks // MOE_CPB) + N_EXPERTS
    pc = tile_chunks.astype(jnp.int32)
    seg = jnp.cumsum(pc, axis=1) - pc
    tile_end = jnp.cumsum(pc.T, axis=1)
    tile_beg = tile_end - pc.T
    tot = tile_end[:, -1]
    nblk = (tot + MOE_CPB - 1) // MOE_CPB
    bend = jnp.cumsum(nblk)
    bbeg = bend - nblk
    bidx = jnp.arange(n_blocks, dtype=jnp.int32)
    be = jnp.minimum(jnp.sum((bend[None, :] <= bidx[:, None]).astype(jnp.int32), axis=1), N_EXPERTS - 1)
    onehot_e = (be[:, None] == jnp.arange(N_EXPERTS, dtype=jnp.int32)[None, :]).astype(jnp.int32)
    pick = lambda vec: jnp.sum(onehot_e * vec[None, :], axis=1)
    q = ((bidx - pick(bbeg)) * MOE_CPB)[:, None] + jnp.arange(MOE_CPB, dtype=jnp.int32)[None, :]
    valid = (q < pick(tot)[:, None]) & (bidx < bend[-1])[:, None]
    rows_of = lambda tbl: jnp.sum(onehot_e[:, :, None] * tbl[None, :, :], axis=1)
    blk_end = rows_of(tile_end)
    base = jnp.arange(n_tiles, dtype=jnp.int32)[None, :] * cap_chunks + seg.T - tile_beg
    blk_base = rows_of(base)
    tile = jnp.sum((blk_end[:, None, :] <= q[:, :, None]).astype(jnp.int32), axis=2)
    tile = jnp.minimum(tile, n_tiles - 1)
    onehot_t = tile[:, :, None] == jnp.arange(n_tiles, dtype=jnp.int32)[None, None, :]
    src = q + jnp.sum(jnp.where(onehot_t, blk_base[:, None, :], 0), axis=2)
    src = jnp.where(valid, src, src[:, 0:1])
    nvalid = jnp.sum(valid, axis=1).astype(jnp.int32)
    src = jnp.where(nvalid[:, None] > 0, src, 0).astype(jnp.int32)
    return be.astype(jnp.int32), nvalid, src.reshape(n_blocks, 1, MOE_CPB)


def _final_kernel(h1_ref, ys_ref, pos_ref, w_ref, g_ref, b_ref, o_ref, *, alpha):
    cap = ys_ref.shape[0]
    n = h1_ref.shape[0]
    slot = lax.broadcasted_iota(jnp.int32, (cap, n), 0)
    pw = (jnp.where(slot == pos_ref[0:1, :], w_ref[0:1, :], 0.0)
          + jnp.where(slot == pos_ref[1:2, :], w_ref[1:2, :], 0.0))
    pw_hi = pw.astype(BF16)
    pw_lo = (pw - pw_hi.astype(F32)).astype(BF16)
    ys = ys_ref[...].astype(BF16)
    tn = (((0,), (0,)), ((), ()))
    moe = (lax.dot_general(pw_hi, ys, tn, preferred_element_type=F32)
           + lax.dot_general(pw_lo, ys, tn, preferred_element_type=F32))
    o_ref[...] = _ln(alpha * h1_ref[...] + moe, g_ref[...], b_ref[...])


def _final(h1, ys, pos, wts, g, b, *, alpha):
    t, d = h1.shape
    tm = MOE_TM
    return pl.pallas_call(
        functools.partial(_final_kernel, alpha=alpha),
        grid=(t // tm,),
        in_specs=[
            pl.BlockSpec((tm, d), lambda i: (i, 0)),
            pl.BlockSpec((MOE_CAP, d), lambda i: (i, 0)),
            pl.BlockSpec((EXPERTS_PER_GROUP, tm), lambda i: (0, i)),
            pl.BlockSpec((EXPERTS_PER_GROUP, tm), lambda i: (0, i)),
            pl.BlockSpec((1, d), lambda i: (0, 0)),
            pl.BlockSpec((1, d), lambda i: (0, 0)),
        ],
        out_specs=pl.BlockSpec((tm, d), lambda i: (i, 0)),
        out_shape=jax.ShapeDtypeStruct((t, d), F32),
        compiler_params=pltpu.CompilerParams(
            dimension_semantics=("parallel",), vmem_limit_bytes=VMEM_LIMIT),
        name="combine_postnorm",
    )(h1, ys, pos, wts, g.reshape(1, d), b.reshape(1, d))


def kernel(x, emb_ln_g, emb_ln_b, w_in, na_rpb, hg_lb, hg_norm_g, w_proj_a, w_proj_b, w_out, ln1_g, ln1_b,
           w_router_group, b_router_group, w_router_expert, b_router_expert, w_gate, w_up, w_down, ln2_g, ln2_b):
    batch, seq, d = x.shape
    depth = w_in.shape[0]
    t = batch * seq
    na_w = na_rpb.shape[1] * NA_HEAD_DIM
    hg_w = hg_norm_g.shape[1]
    hg_heads = hg_w // HG_DK
    alpha = float((2 * depth) ** 0.25)
    lb_all = jnp.cumsum(jax.nn.softmax(hg_lb.astype(F32), axis=1), axis=1)

    assert depth == 1, depth
    l = 0
    x2d = x.reshape(t, d)
    na, hg, rest = _inproj(x2d, emb_ln_g, emb_ln_b, w_in[l].astype(BF16), lb_all[:, l], n_na=3 * na_w, hg_w=hg_w,
                           q_cols=na_w, q_scale=float(NA_HEAD_DIM ** -0.5 * LOG2_E))
    bias = _na_bias_tables(na_rpb[l], seq // GRID_W)
    a = _na(na.reshape(batch, seq, 3 * na_w), bias, batch=batch, seq=seq).reshape(t, na_w)
    o2 = _hgrn(hg, rest, batch=batch, seq=seq, heads=hg_heads)

    wr_t = jnp.zeros((LANES, d), F32)
    wr_t = wr_t.at[0:N_GROUPS].set(w_router_group[l].T.astype(F32))
    wr_t = wr_t.at[EXPERTS_PER_GROUP:EXPERTS_PER_GROUP + N_EXPERTS].set(w_router_expert[l].T.astype(F32))
    br = jnp.zeros((LANES,), F32)
    br = br.at[0:N_GROUPS].set(b_router_group[l].astype(F32))
    br = br.at[EXPERTS_PER_GROUP:EXPERTS_PER_GROUP + N_EXPERTS].set(b_router_expert[l].astype(F32))
    br_col = jnp.broadcast_to(br[:, None], (LANES, LANES))

    h1, pos, route_w, xs, tile_chunks = _merge(
        a, o2, rest, x2d, emb_ln_g, emb_ln_b, hg_norm_g[l], w_proj_a[l].astype(BF16), w_proj_b[l].astype(BF16),
        w_out[l].astype(BF16), ln1_g[l], ln1_b[l], wr_t, br_col, heads=hg_heads, alpha=alpha)

    plan = _chunk_plan(tile_chunks[:, 0].reshape(t // MOE_TM, N_EXPERTS))
    ys = _experts(xs, *plan, w_gate[l].astype(BF16), w_up[l].astype(BF16), w_down[l].astype(BF16))
    h2 = _final(h1, ys, pos, route_w, ln2_g[l], ln2_b[l], alpha=alpha)
    return h2.reshape(batch, seq, d)
```

```python
import functools

import numpy as np
import jax
import jax.numpy as jnp
from jax import lax
from jax.experimental import pallas as pl
from jax.experimental.pallas import tpu as pltpu

F32 = jnp.float32
BF16 = jnp.bfloat16

GRID_W = 64
NA_HEAD_DIM = 64
NA_KH = 8
NA_KW = 16
HG_DK = 128
N_GROUPS = 4
EXPERTS_PER_GROUP = 8
N_EXPERTS = N_GROUPS * EXPERTS_PER_GROUP
TOP_K = 2
LN_EPS = 1e-5
RMS_EPS = 1e-6

LANES = 128
SUBLANES = 8
VMEM_LIMIT = 56 * 1024 * 1024
NEG_BIG = -1e30
LOG2_E = 1.4426950408889634

NA_QROWS = 4
NA_KROWS = NA_QROWS + NA_KH
HG_CHUNK = 128
HG_LEVELS = 7
HG_PAIR = 2
MOE_TM = 256
MOE_CHUNK = 8
MOE_CAP = TOP_K * MOE_TM + N_EXPERTS * MOE_CHUNK
MOE_BM = 256
MOE_CPB = MOE_BM // MOE_CHUNK
MOE_DMA_GROUP = 8


def _ln(x, g, b):
    mu = jnp.mean(x, axis=-1, keepdims=True)
    xc = x - mu
    var = jnp.mean(xc * xc, axis=-1, keepdims=True)
    return xc * lax.rsqrt(var + LN_EPS) * g + b


def _sigmoid(x):
    return 0.5 * jnp.tanh(0.5 * x) + 0.5


def _inproj_kernel(x_ref, g_ref, b_ref, w_ref, lb_ref, h0_ref, na_ref, hg_ref, rest_ref, *,
                   n_na, hg_w, q_cols, q_scale):
    h0 = _ln(x_ref[...], g_ref[...], b_ref[...])
    h0_ref[...] = h0
    h = h0.astype(BF16)
    d_in = w_ref.shape[1]
    cn = hg_w
    for c0 in range(0, d_in, cn):
        u = jnp.dot(h, w_ref[:, c0:c0 + cn], preferred_element_type=F32)
        if c0 < q_cols:
            u = u * q_scale
        if c0 < n_na:
            na_ref[:, c0:c0 + cn] = u.astype(na_ref.dtype)
        elif c0 == n_na:
            hg_ref[:, 0:cn] = (u * _sigmoid(u)).astype(hg_ref.dtype)
        elif c0 < n_na + 3 * hg_w:
            d = (c0 - n_na) // hg_w - 1
            lb = lb_ref[d]
            forget = lb + (1.0 - lb) * _sigmoid(u)
            logf = jnp.log(forget)
            hi = logf.astype(BF16)
            hg_ref[:, (1 + d) * cn:(2 + d) * cn] = (1.0 - forget).astype(hg_ref.dtype)
            hg_ref[:, (3 + 2 * d) * cn:(4 + 2 * d) * cn] = hi
            hg_ref[:, (4 + 2 * d) * cn:(5 + 2 * d) * cn] = (logf - hi.astype(F32)).astype(BF16)
        else:
            o = c0 - n_na - 3 * hg_w
            rest_ref[:, o:o + cn] = u.astype(rest_ref.dtype)


def _inproj(x2d, g, b, w_bf16, lb, *, n_na, hg_w, q_cols, q_scale, tm=512):
    t, d = x2d.shape
    d_in = w_bf16.shape[1]
    n_rest = d_in - n_na - 3 * hg_w
    kern = functools.partial(_inproj_kernel, n_na=n_na, hg_w=hg_w, q_cols=q_cols, q_scale=q_scale)
    return pl.pallas_call(
        kern,
        grid=(t // tm,),
        in_specs=[
            pl.BlockSpec((tm, d), lambda i: (i, 0)),
            pl.BlockSpec((1, d), lambda i: (0, 0)),
            pl.BlockSpec((1, d), lambda i: (0, 0)),
            pl.BlockSpec((d, d_in), lambda i: (0, 0), pipeline_mode=pl.Buffered(1)),
            pl.BlockSpec((2, 1, hg_w), lambda i: (0, 0, 0)),
        ],
        out_specs=[
            pl.BlockSpec((tm, d), lambda i: (i, 0)),
            pl.BlockSpec((tm, n_na), lambda i: (i, 0)),
            pl.BlockSpec((tm, 7 * hg_w), lambda i: (i, 0)),
            pl.BlockSpec((tm, n_rest), lambda i: (i, 0)),
        ],
        out_shape=[
            jax.ShapeDtypeStruct((t, d), F32),
            jax.ShapeDtypeStruct((t, n_na), BF16),
            jax.ShapeDtypeStruct((t, 7 * hg_w), BF16),
            jax.ShapeDtypeStruct((t, n_rest), BF16),
        ],
        compiler_params=pltpu.CompilerParams(
            dimension_semantics=("parallel",), vmem_limit_bytes=VMEM_LIMIT),
        name="ln_inproj",
    )(x2d, g.reshape(1, d), b.reshape(1, d), w_bf16, lb.reshape(2, 1, hg_w))


def _na_bias_tables(rpb, rows):
    nq = NA_QROWS * GRID_W
    nk = NA_KROWS * GRID_W
    qi = np.arange(nq)
    kj = np.arange(nk)
    qc = (qi % GRID_W)[:, None]
    kc = (kj % GRID_W)[None, :]
    cs = np.clip(qc - NA_KW // 2, 0, GRID_W - NA_KW)
    valid_c = (kc >= cs) & (kc < cs + NA_KW)
    n_rel_r, n_rel_c = 2 * NA_KH - 1, 2 * NA_KW - 1
    wc = np.arange(GRID_W)
    cidx = np.clip(wc[None, :] - wc[:, None] + NA_KW - 1, 0, n_rel_c - 1)
    onehot_c = (cidx[None] == np.arange(n_rel_c)[:, None, None]).astype(np.float32)
    ridx_all, valid_all = [], []
    for r0 in (0, NA_QROWS, rows - NA_QROWS):
        start = int(np.clip(r0 - NA_KH // 2, 0, rows - NA_KROWS))
        qr = (r0 + qi // GRID_W)[:, None]
        kr = (start + kj // GRID_W)[None, :]
        rs = np.clip(qr - NA_KH // 2, 0, rows - NA_KH)
        valid_all.append((kr >= rs) & (kr < rs + NA_KH) & valid_c)
        qr_s = r0 + np.arange(NA_QROWS)[:, None]
        kr_s = start + np.arange(NA_KROWS)[None, :]
        ridx_all.append(np.clip(kr_s - qr_s + NA_KH - 1, 0, n_rel_r - 1))
    valid = np.stack(valid_all)
    rows_sel = rpb.astype(F32)[:, np.stack(ridx_all), :]
    bias = jnp.einsum('hcqkj,jxy->chqxky', rows_sel, onehot_c, precision=lax.Precision.HIGHEST)
    bias = bias.reshape(3, rpb.shape[0], nq, nk) * LOG2_E
    return jnp.where(valid[:, None], bias, NEG_BIG)


def _na_kernel(q_ref, k_ref, v_ref, bias_ref, o_ref, *, rows):
    rb = pl.program_id(2)
    start = jnp.clip(rb * NA_QROWS - NA_KH // 2, 0, rows - NA_KROWS)
    kstart = pl.multiple_of(start * GRID_W, NA_QROWS * GRID_W)
    nk = NA_KROWS * GRID_W
    kwin = k_ref[0, pl.ds(kstart, nk), :]
    vwin = v_ref[0, pl.ds(kstart, nk), :]
    q = q_ref[0]
    lane = lax.broadcasted_iota(jnp.int32, (1, LANES), 1)
    first = lane < NA_HEAD_DIM
    zero = jnp.zeros_like(q)
    nq = q.shape[0]
    half = nq // 2
    work = [(hh, r0) for hh in range(2) for r0 in (0, half)]
    qs = [jnp.where(first, q, zero), jnp.where(first, zero, q)]
    ss = [lax.dot_general(qs[hh][r0:r0 + half], kwin, (((1,), (1,)), ((), ())), preferred_element_type=F32)
          + bias_ref[0, 0, hh, r0:r0 + half, :] for hh, r0 in work]
    ps = [jnp.exp2(s - jnp.max(s, axis=-1, keepdims=True)) for s in ss]
    ls = [jnp.sum(p, axis=-1, keepdims=True) for p in ps]
    os = [jnp.dot(p.astype(BF16), vwin, preferred_element_type=F32) / l for p, l in zip(ps, ls)]
    o_ref[0] = jnp.where(first, jnp.concatenate(os[0:2], axis=0),
                         jnp.concatenate(os[2:4], axis=0)).astype(o_ref.dtype)


def _na(qkv, bias, *, batch, seq):
    rows = seq // GRID_W
    width = qkv.shape[-1] // 3
    pairs = width // LANES
    nq = NA_QROWS * GRID_W
    nk = NA_KROWS * GRID_W
    nrb = rows // NA_QROWS
    bias5 = bias.reshape(3, pairs, 2, nq, nk)

    def bias_map(b, p, r):
        cls = jnp.where(r == 0, 0, jnp.where(r == nrb - 1, 2, 1))
        return (cls, p, 0, 0, 0)

    return pl.pallas_call(
        functools.partial(_na_kernel, rows=rows),
        grid=(batch, pairs, nrb),
        in_specs=[
            pl.BlockSpec((1, nq, LANES), lambda b, p, r: (b, r, p)),
            pl.BlockSpec((1, seq, LANES), lambda b, p, r: (b, 0, pairs + p)),
            pl.BlockSpec((1, seq, LANES), lambda b, p, r: (b, 0, 2 * pairs + p)),
            pl.BlockSpec((1, 1, 2, nq, nk), bias_map),
        ],
        out_specs=pl.BlockSpec((1, nq, LANES), lambda b, p, r: (b, r, p)),
        out_shape=jax.ShapeDtypeStruct((batch, seq, width), BF16),
        compiler_params=pltpu.CompilerParams(
            dimension_semantics=("parallel", "parallel", "arbitrary"), vmem_limit_bytes=VMEM_LIMIT),
        name="neighbourhood_attention",
    )(qkv, qkv, qkv, bias5)


def _hg_tables():
    c, nl = HG_CHUNK, HG_LEVELS
    a = np.zeros((2, c * (nl + 1) + 8, c), np.float32)
    lvl = np.full((2, c, c), -1, np.int32)
    for rev in (0, 1):
        for t in range(c):
            if rev == 0:
                a[rev, t, :t + 1] = 1.0
            else:
                a[rev, t, t:] = 1.0
            lvl[rev, t, t] = nl
            for s in range(c):
                if (s < t and rev == 0) or (s > t and rev == 1):
                    lvl[rev, t, s] = int(np.floor(np.log2(t ^ s)))
            for l in range(nl):
                m = 1 << l
                mid = (t & ~(2 * m - 1)) + m
                upper = (t >> l) & 1
                row = c * (l + 1) + t
                if rev == 0:
                    lo, hi = (mid, t) if upper else (t + 1, mid - 1)
                else:
                    lo, hi = (mid, t - 1) if upper else (t, mid - 1)
                if hi >= lo:
                    a[rev, row, lo:hi + 1] = 1.0
        a[rev, c * (nl + 1):, :] = 1.0
    return a, lvl


def _hg_kernel(q_ref, k_ref, up_ref, dn_ref, ghi_ref, glo_ref, i_ref, a_ref, lvl_ref, o_ref, st_ref, *,
               heads, n_chunks):
    rev = pl.program_id(1)
    c, nl = HG_CHUNK, HG_LEVELS

    @pl.when(pl.program_id(2) == 0)
    def _():
        st_ref[...] = jnp.zeros_like(st_ref)

    a_mat = a_ref[0]
    lvl = lvl_ref[0]
    w = heads * LANES
    sub = lax.broadcasted_iota(jnp.int32, (SUBLANES, w), 0)
    nt = (((1,), (1,)), ((), ()))
    tn = (((0,), (0,)), ((), ()))
    head_cols = [slice(h * LANES, (h + 1) * LANES) for h in range(heads)]

    def level_operand(l, up, dn):
        m = 1 << l
        if m >= SUBLANES:
            return jnp.concatenate([(up if k & 1 else dn)[k * m:(k + 1) * m] for k in range(c // m)], axis=0)
        upper = ((sub >> l) & 1) == 1
        shape3 = (c // SUBLANES, SUBLANES, w)
        return jnp.where(upper[None], up.reshape(shape3), dn.reshape(shape3)).reshape(c, w)

    def chunk(j):
        cj = jnp.where(rev == 1, n_chunks - 1 - j, j)
        r0 = pl.multiple_of(cj * c, c)
        rows = pl.ds(r0, c)
        qb = q_ref[rows, :]
        kb = k_ref[rows, :]
        q = qb.astype(F32)
        kk = kb.astype(F32)
        up = up_ref[rows, :].astype(F32)
        dn = dn_ref[rows, :].astype(F32)
        v = i_ref[rows, :]
        e = jnp.dot(a_mat, jnp.concatenate([ghi_ref[rows, :], glo_ref[rows, :]], axis=0),
                    preferred_element_type=F32)
        b = e[0:c]
        btot = e[c * (nl + 1):c * (nl + 1) + 1]
        qd = (q * jnp.exp(b)).astype(BF16)
        khat = (kk * jnp.exp(btot - b)).astype(BF16)
        decay = jnp.exp(btot)
        xs = [(level_operand(l, up, dn) * jnp.exp(e[c * (l + 1):c * (l + 2)])).astype(BF16) for l in range(nl)]
        attns = [jnp.where(lvl == nl, lax.dot_general(qb[:, hc], kb[:, hc], nt, preferred_element_type=F32), 0.0)
                 for hc in head_cols]
        for l in range(nl):
            x = xs[l]
            attns = [jnp.where(lvl == l, lax.dot_general(x[:, hc], x[:, hc], nt, preferred_element_type=F32), at)
                     for hc, at in zip(head_cols, attns)]
        intra = [jnp.dot(at.astype(BF16), v[:, hc], preferred_element_type=F32)
                 for hc, at in zip(head_cols, attns)]
        update = [lax.dot_general(v[:, hc], khat[:, hc], tn, preferred_element_type=F32) for hc in head_cols]
        return rows, qd, decay, intra, update

    def chunk_pair(jp, carry):
        parts = [chunk(jp * HG_PAIR + u) for u in range(HG_PAIR)]
        sts = [st_ref[h] for h in range(heads)]
        for rows, qd, decay, intra, update in parts:
            for h, hc in enumerate(head_cols):
                o_ref[0, rows, hc] = intra[h] + lax.dot_general(qd[:, hc], sts[h].astype(BF16), nt,
                                                                preferred_element_type=F32)
            sts = [st * decay[:, hc] + up_h for st, hc, up_h in zip(sts, head_cols, update)]
        for h in range(heads):
            st_ref[h] = sts[h]
        return carry

    lax.fori_loop(0, n_chunks // HG_PAIR, chunk_pair, 0)


def _hgrn(hg, rest, *, batch, seq, heads, cb=512):
    t = batch * seq
    w = heads * LANES
    nsb = seq // cb
    a_np, lvl_np = _hg_tables()
    a_np = np.concatenate([a_np, a_np], axis=-1)
    a_mat = jnp.asarray(a_np, BF16)
    lvl = jnp.asarray(lvl_np)

    def rowblk(b, d, s):
        return b * nsb + jnp.where(d == 1, nsb - 1 - s, s)

    def col(fn):
        return pl.BlockSpec((cb, w), lambda b, d, s: (rowblk(b, d, s), fn(d)))

    return pl.pallas_call(
        functools.partial(_hg_kernel, heads=heads, n_chunks=cb // HG_CHUNK),
        grid=(batch, 2, nsb),
        in_specs=[
            col(lambda d: 0),
            col(lambda d: 1 + d),
            col(lambda d: 2 * d),
            col(lambda d: 1 - d),
            col(lambda d: 3 + 2 * d),
            col(lambda d: 4 + 2 * d),
            col(lambda d: 0),
            pl.BlockSpec((1,) + a_np.shape[1:], lambda b, d, s: (d, 0, 0)),
            pl.BlockSpec((1,) + lvl_np.shape[1:], lambda b, d, s: (d, 0, 0)),
        ],
        out_specs=pl.BlockSpec((1, cb, w), lambda b, d, s: (d, rowblk(b, d, s), 0)),
        out_shape=jax.ShapeDtypeStruct((2, t, w), F32),
        scratch_shapes=[pltpu.VMEM((heads, LANES, LANES), F32)],
        compiler_params=pltpu.CompilerParams(
            dimension_semantics=("parallel", "parallel", "arbitrary"), vmem_limit_bytes=VMEM_LIMIT),
        name="hgrn2_scan",
    )(hg, hg, hg, hg, hg, hg, rest, a_mat, lvl)


def _first_index_of_max(x, idx):
    m = jnp.max(x, axis=0, keepdims=True)
    i = jnp.min(jnp.where(x == m, idx, EXPERTS_PER_GROUP), axis=0, keepdims=True)
    return m, i


def _merge_kernel(a_ref, o2_ref, g_ref, ga_ref, gb_ref, h0_ref, ng_ref,
                  wa_ref, wb_ref, wo_ref, ln1g_ref, ln1b_ref, wrh_ref, wrl_ref, br_ref,
                  h1_ref, ri_ref, rw_ref, xs_ref, pc_ref, *, heads, alpha):
    o = o2_ref[0] + o2_ref[1]
    gate = g_ref[...].astype(F32)
    parts = []
    for h in range(heads):
        cols = slice(h * LANES, (h + 1) * LANES)
        oh = o[:, cols]
        ms = jnp.mean(oh * oh, axis=-1, keepdims=True)
        gh = gate[:, cols]
        parts.append((oh * lax.rsqrt(ms + RMS_EPS) * ng_ref[:, cols] * (gh * _sigmoid(gh))).astype(BF16))
    c = jnp.concatenate(parts, axis=-1)
    pa = jnp.dot(a_ref[...], wa_ref[...], preferred_element_type=F32)
    pb = jnp.dot(c, wb_ref[...], preferred_element_type=F32)
    merged = _sigmoid(ga_ref[...].astype(F32)) * pa + _sigmoid(gb_ref[...].astype(F32)) * pb
    y = jnp.dot(merged.astype(BF16), wo_ref[...], preferred_element_type=F32)
    h1 = _ln(alpha * h0_ref[...] + y, ln1g_ref[...], ln1b_ref[...])
    h1_ref[...] = h1

    nt = (((1,), (1,)), ((), ()))
    h_hi = h1.astype(BF16)
    h_lo = (h1 - h_hi.astype(F32)).astype(BF16)
    logits = (lax.dot_general(wrh_ref[...], h_hi, nt, preferred_element_type=F32)
              + lax.dot_general(wrh_ref[...], h_lo, nt, preferred_element_type=F32)
              + lax.dot_general(wrl_ref[...], h_hi, nt, preferred_element_type=F32) + br_ref[:, 0:1])
    n = logits.shape[1]
    idx = lax.broadcasted_iota(jnp.int32, (EXPERTS_PER_GROUP, n), 0)
    glog = jnp.where(idx < N_GROUPS, logits[0:EXPERTS_PER_GROUP], -jnp.inf)
    gmax, gsel = _first_index_of_max(glog, idx)
    gw = 1.0 / jnp.sum(jnp.exp(glog - gmax), axis=0, keepdims=True)
    elog = logits[EXPERTS_PER_GROUP:2 * EXPERTS_PER_GROUP]
    for gi in range(1, N_GROUPS):
        lo = EXPERTS_PER_GROUP * (gi + 1)
        elog = jnp.where(gsel == gi, logits[lo:lo + EXPERTS_PER_GROUP], elog)
    ee = jnp.exp(elog - jnp.max(elog, axis=0, keepdims=True))
    p = ee / jnp.sum(ee, axis=0, keepdims=True)
    p1, i1 = _first_index_of_max(p, idx)
    p2, i2 = _first_index_of_max(jnp.where(idx == i1, -1.0, p), idx)
    denom = p1 + p2
    e1 = gsel * EXPERTS_PER_GROUP + i1
    e2 = gsel * EXPERTS_PER_GROUP + i2
    rw_ref[...] = jnp.where(idx == 0, gw * p1 / denom, jnp.where(idx == 1, gw * p2 / denom, 0.0))

    ex = lax.broadcasted_iota(jnp.int32, (N_EXPERTS, n), 0)
    m1 = ex == e1
    m2 = ex == e2
    member = jnp.where(m1, 1.0, jnp.where(m2, 1.0, 0.0))
    before = (lax.broadcasted_iota(jnp.int32, (n, n), 0) < lax.broadcasted_iota(jnp.int32, (n, n), 1))
    rank = jnp.dot(member.astype(BF16), jnp.where(before, 1.0, 0.0).astype(BF16), preferred_element_type=F32)
    count = jnp.sum(member, axis=1, keepdims=True)
    chunks = jnp.floor((count + (MOE_CHUNK - 1)) * (1.0 / MOE_CHUNK))
    chunks_b = jnp.broadcast_to(chunks, (N_EXPERTS, LANES))
    lower = (lax.broadcasted_iota(jnp.int32, (N_EXPERTS, N_EXPERTS), 1)
             < lax.broadcasted_iota(jnp.int32, (N_EXPERTS, N_EXPERTS), 0))
    seg = jnp.dot(jnp.where(lower, 1.0, 0.0), chunks_b, precision=lax.Precision.HIGHEST,
                  preferred_element_type=F32)
    posmat = seg[:, 0:1] * MOE_CHUNK + rank
    pos1 = jnp.sum(jnp.where(m1, posmat, 0.0), axis=0, keepdims=True).astype(jnp.int32)
    pos2 = jnp.sum(jnp.where(m2, posmat, 0.0), axis=0, keepdims=True).astype(jnp.int32)
    ri_ref[...] = jnp.where(idx == 0, pos1, jnp.where(idx == 1, pos2, 0))
    pc_ref[...] = chunks_b
    slot = lax.broadcasted_iota(jnp.int32, (xs_ref.shape[0], n), 0)
    perm = jnp.where(slot == pos1, 1.0, jnp.where(slot == pos2, 1.0, 0.0)).astype(BF16)
    xs_ref[...] = jnp.dot(perm, h_hi, preferred_element_type=F32)


def _merge(a, o2, rest, h0, norm_g, wa, wb, wo, ln1g, ln1b, wr_t, br_col, *, heads, alpha):
    t, d = h0.shape
    w = heads * LANES
    wa_w = a.shape[1]
    tm = MOE_TM
    n_tiles = t // tm
    wr_hi = wr_t.astype(BF16)
    wr_lo = (wr_t - wr_hi.astype(F32)).astype(BF16)
    full = lambda shape: pl.BlockSpec(shape, lambda i: (0,) * len(shape))
    return pl.pallas_call(
        functools.partial(_merge_kernel, heads=heads, alpha=alpha),
        grid=(t // tm,),
        in_specs=[
            pl.BlockSpec((tm, wa_w), lambda i: (i, 0)),
            pl.BlockSpec((2, tm, w), lambda i: (0, i, 0)),
            pl.BlockSpec((tm, w), lambda i: (i, 1)),
            pl.BlockSpec((tm, d), lambda i: (i, 1)),
            pl.BlockSpec((tm, d), lambda i: (i, 2)),
            pl.BlockSpec((tm, d), lambda i: (i, 0)),
            full((1, w)),
            full(wa.shape), full(wb.shape), full(wo.shape),
            full((1, d)), full((1, d)),
            full(wr_t.shape), full(wr_t.shape), full(br_col.shape),
        ],
        out_specs=[
            pl.BlockSpec((tm, d), lambda i: (i, 0)),
            pl.BlockSpec((EXPERTS_PER_GROUP, tm), lambda i: (0, i)),
            pl.BlockSpec((EXPERTS_PER_GROUP, tm), lambda i: (0, i)),
            pl.BlockSpec((MOE_CAP, d), lambda i: (i, 0)),
            pl.BlockSpec((N_EXPERTS, LANES), lambda i: (i, 0)),
        ],
        out_shape=[
            jax.ShapeDtypeStruct((t, d), F32),
            jax.ShapeDtypeStruct((EXPERTS_PER_GROUP, t), jnp.int32),
            jax.ShapeDtypeStruct((EXPERTS_PER_GROUP, t), F32),
            jax.ShapeDtypeStruct((n_tiles * MOE_CAP, d), F32),
            jax.ShapeDtypeStruct((n_tiles * N_EXPERTS, LANES), F32),
        ],
        compiler_params=pltpu.CompilerParams(
            dimension_semantics=("parallel",), vmem_limit_bytes=VMEM_LIMIT),
        name="merge_outproj_router",
    )(a, o2, rest, rest, rest, h0, norm_g.reshape(1, w),
      wa, wb, wo, ln1g.reshape(1, d), ln1b.reshape(1, d), wr_hi, wr_lo, br_col)


def _expert_kernel(be_ref, nv_ref, cur_ref, nxt_ref, xy_in, wg_ref, wu_ref, wd_ref, xy_out,
                   xbuf, ybuf, wg_s, wu_s, wd_s, gsem, ssem):
    i = pl.program_id(0)
    n = pl.num_programs(0)
    slot = i % 2
    nv = nv_ref[i]

    def gather_copy(tbl_ref, s, buf_slot):
        src = pl.multiple_of(tbl_ref[0, 0, s] * MOE_CHUNK, MOE_CHUNK)
        dst = pl.multiple_of(s * MOE_CHUNK, MOE_CHUNK)
        return pltpu.make_async_copy(xy_in.at[pl.ds(src, MOE_CHUNK), :],
                                     xbuf.at[buf_slot, pl.ds(dst, MOE_CHUNK), :], gsem.at[buf_slot])

    def scatter_copy(s, buf_slot):
        src = pl.multiple_of(s * MOE_CHUNK, MOE_CHUNK)
        dst = pl.multiple_of(cur_ref[0, 0, s] * MOE_CHUNK, MOE_CHUNK)
        return pltpu.make_async_copy(ybuf.at[buf_slot, pl.ds(src, MOE_CHUNK), :],
                                     xy_out.at[pl.ds(dst, MOE_CHUNK), :], ssem.at[buf_slot])

    def for_each_chunk(count, fn):
        for g0 in range(0, MOE_CPB, MOE_DMA_GROUP):
            @pl.when(count >= g0 + MOE_DMA_GROUP)
            def _():
                for s in range(g0, g0 + MOE_DMA_GROUP):
                    fn(s)

        def body(s, c):
            fn(s)
            return c
        lax.fori_loop(count // MOE_DMA_GROUP * MOE_DMA_GROUP, count, body, 0)

    def start_gathers(tbl_ref, buf_slot):
        for s in range(MOE_CPB):
            gather_copy(tbl_ref, s, buf_slot).start()

    def wait_scatters(count, buf_slot):
        for_each_chunk(count, lambda s: scatter_copy(0, buf_slot).wait())

    @pl.when((i == 0) & (nv > 0))
    def _():
        start_gathers(cur_ref, slot)

    nxt = jnp.minimum(i + 1, n - 1)

    @pl.when((i + 1 < n) & (nv_ref[nxt] > 0))
    def _():
        start_gathers(nxt_ref, 1 - slot)

    @pl.when(i >= 2)
    def _():
        wait_scatters(nv_ref[jnp.maximum(i - 2, 0)], slot)

    @pl.when(nv > 0)
    def _():
        for _ in range(MOE_CPB):
            gather_copy(cur_ref, 0, slot).wait()

        @pl.when((i == 0) | (be_ref[i] != be_ref[jnp.maximum(i - 1, 0)]))
        def _():
            wg_s[...] = wg_ref[0].astype(BF16)
            wu_s[...] = wu_ref[0].astype(BF16)
            wd_s[...] = wd_ref[0].astype(BF16)

        x = xbuf[slot].astype(BF16)
        gate = jnp.dot(x, wg_s[...], preferred_element_type=F32)
        up = jnp.dot(x, wu_s[...], preferred_element_type=F32)
        hid = (gate * _sigmoid(gate) * up).astype(BF16)
        ybuf[slot] = jnp.dot(hid, wd_s[...], preferred_element_type=F32)

        for_each_chunk(nv, lambda s: scatter_copy(s, slot).start())

    @pl.when(i == n - 1)
    def _():
        @pl.when(i >= 1)
        def _():
            wait_scatters(nv_ref[jnp.maximum(i - 1, 0)], 1 - slot)
        wait_scatters(nv, slot)


def _experts(xy, block_expert, block_nvalid, table, wg, wu, wd):
    rows, d = xy.shape
    n_blocks = table.shape[0]
    ne, _, dff = wg.shape
    return pl.pallas_call(
        _expert_kernel,
        grid_spec=pltpu.PrefetchScalarGridSpec(
            num_scalar_prefetch=2,
            grid=(n_blocks,),
            in_specs=[
                pl.BlockSpec((1, 1, MOE_CPB), lambda i, be, nv: (i, 0, 0), memory_space=pltpu.SMEM),
                pl.BlockSpec((1, 1, MOE_CPB), lambda i, be, nv: (jnp.minimum(i + 1, n_blocks - 1), 0, 0),
                             memory_space=pltpu.SMEM),
                pl.BlockSpec(memory_space=pl.ANY),
                pl.BlockSpec((1, d, dff), lambda i, be, nv: (be[i], 0, 0)),
                pl.BlockSpec((1, d, dff), lambda i, be, nv: (be[i], 0, 0)),
                pl.BlockSpec((1, dff, d), lambda i, be, nv: (be[i], 0, 0)),
            ],
            out_specs=pl.BlockSpec(memory_space=pl.ANY),
            scratch_shapes=[
                pltpu.VMEM((2, MOE_BM, d), F32),
                pltpu.VMEM((2, MOE_BM, d), F32),
                pltpu.VMEM((d, dff), BF16),
                pltpu.VMEM((d, dff), BF16),
                pltpu.VMEM((dff, d), BF16),
                pltpu.SemaphoreType.DMA((2,)),
                pltpu.SemaphoreType.DMA((2,)),
            ],
        ),
        out_shape=jax.ShapeDtypeStruct((rows, d), F32),
        input_output_aliases={4: 0},
        compiler_params=pltpu.CompilerParams(
            dimension_semantics=("arbitrary",), vmem_limit_bytes=VMEM_LIMIT),
        name="expert_ffn",
    )(block_expert, block_nvalid, table, table, xy, wg, wu, wd)


def _chunk_plan(tile_chunks):
    n_tiles = tile_chunks.shape[0]
    cap_chunks = MOE_CAP // MOE_CHUNK
    max_tile_chunks = (TOP_K * MOE_TM + N_EXPERTS * (MOE_CHUNK - 1)) // MOE_CHUNK
    n_blocks = -(-n_tiles * max_tile_chunks // MOE_CPB) + N_EXPERTS
    pc = tile_chunks.astype(jnp.int32)
    seg = jnp.cumsum(pc, axis=1) - pc
    tile_end = jnp.cumsum(pc.T, axis=1)
    tile_beg = tile_end - pc.T
    tot = tile_end[:, -1]
    nblk = (tot + MOE_CPB - 1) // MOE_CPB
    bend = jnp.cumsum(nblk)
    bbeg = bend - nblk
    bidx = jnp.arange(n_blocks, dtype=jnp.int32)
    be = jnp.minimum(jnp.sum((bend[None, :] <= bidx[:, None]).astype(jnp.int32), axis=1), N_EXPERTS - 1)
    onehot_e = (be[:, None] == jnp.arange(N_EXPERTS, dtype=jnp.int32)[None, :]).astype(jnp.int32)
    pick = lambda vec: jnp.sum(onehot_e * vec[None, :], axis=1)
    q = ((bidx - pick(bbeg)) * MOE_CPB)[:, None] + jnp.arange(MOE_CPB, dtype=jnp.int32)[None, :]
    valid = (q < pick(tot)[:, None]) & (bidx < bend[-1])[:, None]
    rows_of = lambda tbl: jnp.sum(onehot_e[:, :, None] * tbl[None, :, :], axis=1)
    blk_end = rows_of(tile_end)
    base = jnp.arange(n_tiles, dtype=jnp.int32)[None, :] * cap_chunks + seg.T - tile_beg
    blk_base = rows_of(base)
    tile = jnp.sum((blk_end[:, None, :] <= q[:, :, None]).astype(jnp.int32), axis=2)
    tile = jnp.minimum(tile, n_tiles - 1)
    onehot_t = tile[:, :, None] == jnp.arange(n_tiles, dtype=jnp.int32)[None, None, :]
    src = q + jnp.sum(jnp.where(onehot_t, blk_base[:, None, :], 0), axis=2)
    src = jnp.where(valid, src, src[:, 0:1])
    nvalid = jnp.sum(valid, axis=1).astype(jnp.int32)
    src = jnp.where(nvalid[:, None] > 0, src, 0).astype(jnp.int32)
    return be.astype(jnp.int32), nvalid, src.reshape(n_blocks, 1, MOE_CPB)


def _final_kernel(h1_ref, ys_ref, pos_ref, w_ref, g_ref, b_ref, o_ref, *, alpha):
    cap = ys_ref.shape[0]
    n = h1_ref.shape[0]
    slot = lax.broadcasted_iota(jnp.int32, (cap, n), 0)
    pw = (jnp.where(slot == pos_ref[0:1, :], w_ref[0:1, :], 0.0)
          + jnp.where(slot == pos_ref[1:2, :], w_ref[1:2, :], 0.0))
    pw_hi = pw.astype(BF16)
    pw_lo = (pw - pw_hi.astype(F32)).astype(BF16)
    ys = ys_ref[...].astype(BF16)
    tn = (((0,), (0,)), ((), ()))
    moe = (lax.dot_general(pw_hi, ys, tn, preferred_element_type=F32)
           + lax.dot_general(pw_lo, ys, tn, preferred_element_type=F32))
    o_ref[...] = _ln(alpha * h1_ref[...] + moe, g_ref[...], b_ref[...])


def _final(h1, ys, pos, wts, g, b, *, alpha):
    t, d = h1.shape
    tm = MOE_TM
    return pl.pallas_call(
        functools.partial(_final_kernel, alpha=alpha),
        grid=(t // tm,),
        in_specs=[
            pl.BlockSpec((tm, d), lambda i: (i, 0)),
            pl.BlockSpec((MOE_CAP, d), lambda i: (i, 0)),
            pl.BlockSpec((EXPERTS_PER_GROUP, tm), lambda i: (0, i)),
            pl.BlockSpec((EXPERTS_PER_GROUP, tm), lambda i: (0, i)),
            pl.BlockSpec((1, d), lambda i: (0, 0)),
            pl.BlockSpec((1, d), lambda i: (0, 0)),
        ],
        out_specs=pl.BlockSpec((tm, d), lambda i: (i, 0)),
        out_shape=jax.ShapeDtypeStruct((t, d), F32),
        compiler_params=pltpu.CompilerParams(
            dimension_semantics=("parallel",), vmem_limit_bytes=VMEM_LIMIT),
        name="combine_postnorm",
    )(h1, ys, pos, wts, g.reshape(1, d), b.reshape(1, d))


def kernel(x, emb_ln_g, emb_ln_b, w_in, na_rpb, hg_lb, hg_norm_g, w_proj_a, w_proj_b, w_out, ln1_g, ln1_b,
           w_router_group, b_router_group, w_router_expert, b_router_expert, w_gate, w_up, w_down, ln2_g, ln2_b):
    batch, seq, d = x.shape
    depth = w_in.shape[0]
    t = batch * seq
    na_w = na_rpb.shape[1] * NA_HEAD_DIM
    hg_w = hg_norm_g.shape[1]
    hg_heads = hg_w // HG_DK
    alpha = float((2 * depth) ** 0.25)
    lb_all = jnp.cumsum(jax.nn.softmax(hg_lb.astype(F32), axis=1), axis=1)

    assert depth == 1, depth
    l = 0
    x2d = x.reshape(t, d)
    h0, na, hg, rest = _inproj(x2d, emb_ln_g, emb_ln_b, w_in[l].astype(BF16), lb_all[:, l], n_na=3 * na_w, hg_w=hg_w,
                           q_cols=na_w, q_scale=float(NA_HEAD_DIM ** -0.5 * LOG2_E))
    bias = _na_bias_tables(na_rpb[l], seq // GRID_W)
    a = _na(na.reshape(batch, seq, 3 * na_w), bias, batch=batch, seq=seq).reshape(t, na_w)
    o2 = _hgrn(hg, rest, batch=batch, seq=seq, heads=hg_heads)

    wr_t = jnp.zeros((LANES, d), F32)
    wr_t = wr_t.at[0:N_GROUPS].set(w_router_group[l].T.astype(F32))
    wr_t = wr_t.at[EXPERTS_PER_GROUP:EXPERTS_PER_GROUP + N_EXPERTS].set(w_router_expert[l].T.astype(F32))
    br = jnp.zeros((LANES,), F32)
    br = br.at[0:N_GROUPS].set(b_router_group[l].astype(F32))
    br = br.at[EXPERTS_PER_GROUP:EXPERTS_PER_GROUP + N_EXPERTS].set(b_router_expert[l].astype(F32))
    br_col = jnp.broadcast_to(br[:, None], (LANES, LANES))

    h1, pos, route_w, xs, tile_chunks = _merge(
        a, o2, rest, h0, hg_norm_g[l], w_proj_a[l].astype(BF16), w_proj_b[l].astype(BF16),
        w_out[l].astype(BF16), ln1_g[l], ln1_b[l], wr_t, br_col, heads=hg_heads, alpha=alpha)

    plan = _chunk_plan(tile_chunks[:, 0].reshape(t // MOE_TM, N_EXPERTS))
    ys = _experts(xs, *plan, w_gate[l], w_up[l], w_down[l])
    h2 = _final(h1, ys, pos, route_w, ln2_g[l], ln2_b[l], alpha=alpha)
    return h2.reshape(batch, seq, d)
```

```python
import functools

import numpy as np
import jax
import jax.numpy as jnp
from jax import lax
from jax.experimental import pallas as pl
from jax.experimental.pallas import tpu as pltpu

F32 = jnp.float32
BF16 = jnp.bfloat16

GRID_W = 64
NA_HEAD_DIM = 64
NA_KH = 8
NA_KW = 16
HG_DK = 128
N_GROUPS = 4
EXPERTS_PER_GROUP = 8
N_EXPERTS = N_GROUPS * EXPERTS_PER_GROUP
TOP_K = 2
LN_EPS = 1e-5
RMS_EPS = 1e-6

LANES = 128
SUBLANES = 8
VMEM_LIMIT = 56 * 1024 * 1024
NEG_BIG = -1e30
LOG2_E = 1.4426950408889634

NA_QROWS = 4
NA_KROWS = NA_QROWS + NA_KH
HG_CHUNK = 128
HG_LEVELS = 7
HG_PAIR = 2
MOE_TM = 512
MOE_CHUNK = 8
MOE_CAP = TOP_K * MOE_TM + N_EXPERTS * MOE_CHUNK
MOE_BM = 256
MOE_CPB = MOE_BM // MOE_CHUNK
MOE_DMA_GROUP = 8


def _ln(x, g, b):
    mu = jnp.mean(x, axis=-1, keepdims=True)
    xc = x - mu
    var = jnp.mean(xc * xc, axis=-1, keepdims=True)
    return xc * lax.rsqrt(var + LN_EPS) * g + b


def _sigmoid(x):
    return 0.5 * jnp.tanh(0.5 * x) + 0.5


def _inproj_kernel(x_ref, g_ref, b_ref, w_ref, lb_ref, h0_ref, na_ref, hg_ref, rest_ref, *,
                   n_na, hg_w, q_cols, q_scale):
    h0 = _ln(x_ref[...], g_ref[...], b_ref[...])
    h0_ref[...] = h0
    h = h0.astype(BF16)
    d_in = w_ref.shape[1]
    cn = hg_w
    for c0 in range(0, d_in, cn):
        u = jnp.dot(h, w_ref[:, c0:c0 + cn], preferred_element_type=F32)
        if c0 < q_cols:
            u = u * q_scale
        if c0 < n_na:
            na_ref[:, c0:c0 + cn] = u.astype(na_ref.dtype)
        elif c0 == n_na:
            hg_ref[:, 0:cn] = (u * _sigmoid(u)).astype(hg_ref.dtype)
        elif c0 < n_na + 3 * hg_w:
            d = (c0 - n_na) // hg_w - 1
            lb = lb_ref[d]
            forget = lb + (1.0 - lb) * _sigmoid(u)
            logf = jnp.log(forget)
            hi = logf.astype(BF16)
            hg_ref[:, (1 + d) * cn:(2 + d) * cn] = (1.0 - forget).astype(hg_ref.dtype)
            hg_ref[:, (3 + 2 * d) * cn:(4 + 2 * d) * cn] = hi
            hg_ref[:, (4 + 2 * d) * cn:(5 + 2 * d) * cn] = (logf - hi.astype(F32)).astype(BF16)
        else:
            o = c0 - n_na - 3 * hg_w
            rest_ref[:, o:o + cn] = u.astype(rest_ref.dtype)


def _inproj(x2d, g, b, w_bf16, lb, *, n_na, hg_w, q_cols, q_scale, tm=512):
    t, d = x2d.shape
    d_in = w_bf16.shape[1]
    n_rest = d_in - n_na - 3 * hg_w
    kern = functools.partial(_inproj_kernel, n_na=n_na, hg_w=hg_w, q_cols=q_cols, q_scale=q_scale)
    return pl.pallas_call(
        kern,
        grid=(t // tm,),
        in_specs=[
            pl.BlockSpec((tm, d), lambda i: (i, 0)),
            pl.BlockSpec((1, d), lambda i: (0, 0)),
            pl.BlockSpec((1, d), lambda i: (0, 0)),
            pl.BlockSpec((d, d_in), lambda i: (0, 0), pipeline_mode=pl.Buffered(1)),
            pl.BlockSpec((2, 1, hg_w), lambda i: (0, 0, 0)),
        ],
        out_specs=[
            pl.BlockSpec((tm, d), lambda i: (i, 0)),
            pl.BlockSpec((tm, n_na), lambda i: (i, 0)),
            pl.BlockSpec((tm, 7 * hg_w), lambda i: (i, 0)),
            pl.BlockSpec((tm, n_rest), lambda i: (i, 0)),
        ],
        out_shape=[
            jax.ShapeDtypeStruct((t, d), F32),
            jax.ShapeDtypeStruct((t, n_na), BF16),
            jax.ShapeDtypeStruct((t, 7 * hg_w), BF16),
            jax.ShapeDtypeStruct((t, n_rest), BF16),
        ],
        compiler_params=pltpu.CompilerParams(
            dimension_semantics=("parallel",), vmem_limit_bytes=VMEM_LIMIT),
        name="ln_inproj",
    )(x2d, g.reshape(1, d), b.reshape(1, d), w_bf16, lb.reshape(2, 1, hg_w))


def _na_bias_tables(rpb, rows):
    nq = NA_QROWS * GRID_W
    nk = NA_KROWS * GRID_W
    qi = np.arange(nq)
    kj = np.arange(nk)
    qc = (qi % GRID_W)[:, None]
    kc = (kj % GRID_W)[None, :]
    cs = np.clip(qc - NA_KW // 2, 0, GRID_W - NA_KW)
    valid_c = (kc >= cs) & (kc < cs + NA_KW)
    n_rel_r, n_rel_c = 2 * NA_KH - 1, 2 * NA_KW - 1
    wc = np.arange(GRID_W)
    cidx = np.clip(wc[None, :] - wc[:, None] + NA_KW - 1, 0, n_rel_c - 1)
    onehot_c = (cidx[None] == np.arange(n_rel_c)[:, None, None]).astype(np.float32)
    ridx_all, valid_all = [], []
    for r0 in (0, NA_QROWS, rows - NA_QROWS):
        start = int(np.clip(r0 - NA_KH // 2, 0, rows - NA_KROWS))
        qr = (r0 + qi // GRID_W)[:, None]
        kr = (start + kj // GRID_W)[None, :]
        rs = np.clip(qr - NA_KH // 2, 0, rows - NA_KH)
        valid_all.append((kr >= rs) & (kr < rs + NA_KH) & valid_c)
        qr_s = r0 + np.arange(NA_QROWS)[:, None]
        kr_s = start + np.arange(NA_KROWS)[None, :]
        ridx_all.append(np.clip(kr_s - qr_s + NA_KH - 1, 0, n_rel_r - 1))
    valid = np.stack(valid_all)
    rows_sel = rpb.astype(F32)[:, np.stack(ridx_all), :]
    bias = jnp.einsum('hcqkj,jxy->chqxky', rows_sel, onehot_c, precision=lax.Precision.HIGHEST)
    bias = bias.reshape(3, rpb.shape[0], nq, nk) * LOG2_E
    return jnp.where(valid[:, None], bias, NEG_BIG)


def _na_kernel(q_ref, k_ref, v_ref, bias_ref, o_ref, *, rows):
    nq = NA_QROWS * GRID_W
    nk = NA_KROWS * GRID_W
    nrb = rows // NA_QROWS
    half = nq // 2
    lane = lax.broadcasted_iota(jnp.int32, (1, LANES), 1)
    first = lane < NA_HEAD_DIM
    work = [(hh, r0) for hh in range(2) for r0 in (0, half)]

    def row_block(rb, carry):
        start = jnp.clip(rb * NA_QROWS - NA_KH // 2, 0, rows - NA_KROWS)
        kstart = pl.multiple_of(start * GRID_W, NA_QROWS * GRID_W)
        qstart = pl.multiple_of(rb * nq, nq)
        cls = jnp.where(rb == 0, 0, jnp.where(rb == nrb - 1, 2, 1))
        kwin = k_ref[0, pl.ds(kstart, nk), :]
        vwin = v_ref[0, pl.ds(kstart, nk), :]
        q = q_ref[0, pl.ds(qstart, nq), :]
        zero = jnp.zeros_like(q)
        qs = [jnp.where(first, q, zero), jnp.where(first, zero, q)]
        ss = [lax.dot_general(qs[hh][r0:r0 + half], kwin, (((1,), (1,)), ((), ())), preferred_element_type=F32)
              + bias_ref[cls, 0, hh, r0:r0 + half, :] for hh, r0 in work]
        ps = [jnp.exp2(s - jnp.max(s, axis=-1, keepdims=True)) for s in ss]
        ls = [jnp.sum(p, axis=-1, keepdims=True) for p in ps]
        os = [jnp.dot(p.astype(BF16), vwin, preferred_element_type=F32) / l for p, l in zip(ps, ls)]
        o_ref[0, pl.ds(qstart, nq), :] = jnp.where(first, jnp.concatenate(os[0:2], axis=0),
                                                   jnp.concatenate(os[2:4], axis=0)).astype(o_ref.dtype)
        return carry

    lax.fori_loop(0, nrb, row_block, 0)


def _na(qkv, bias, *, batch, seq):
    rows = seq // GRID_W
    width = qkv.shape[-1] // 3
    pairs = width // LANES
    nq = NA_QROWS * GRID_W
    nk = NA_KROWS * GRID_W
    bias5 = bias.reshape(3, pairs, 2, nq, nk)
    return pl.pallas_call(
        functools.partial(_na_kernel, rows=rows),
        grid=(pairs, batch),
        in_specs=[
            pl.BlockSpec((1, seq, LANES), lambda p, b: (b, 0, p)),
            pl.BlockSpec((1, seq, LANES), lambda p, b: (b, 0, pairs + p)),
            pl.BlockSpec((1, seq, LANES), lambda p, b: (b, 0, 2 * pairs + p)),
            pl.BlockSpec((3, 1, 2, nq, nk), lambda p, b: (0, p, 0, 0, 0)),
        ],
        out_specs=pl.BlockSpec((1, seq, LANES), lambda p, b: (b, 0, p)),
        out_shape=jax.ShapeDtypeStruct((batch, seq, width), BF16),
        compiler_params=pltpu.CompilerParams(
            dimension_semantics=("parallel", "parallel"), vmem_limit_bytes=VMEM_LIMIT),
        name="neighbourhood_attention",
    )(qkv, qkv, qkv, bias5)


def _hg_tables():
    c, nl = HG_CHUNK, HG_LEVELS
    a = np.zeros((2, c * (nl + 1) + 8, c), np.float32)
    lvl = np.full((2, c, c), -1, np.int32)
    for rev in (0, 1):
        for t in range(c):
            if rev == 0:
                a[rev, t, :t + 1] = 1.0
            else:
                a[rev, t, t:] = 1.0
            lvl[rev, t, t] = nl
            for s in range(c):
                if (s < t and rev == 0) or (s > t and rev == 1):
                    lvl[rev, t, s] = int(np.floor(np.log2(t ^ s)))
            for l in range(nl):
                m = 1 << l
                mid = (t & ~(2 * m - 1)) + m
                upper = (t >> l) & 1
                row = c * (l + 1) + t
                if rev == 0:
                    lo, hi = (mid, t) if upper else (t + 1, mid - 1)
                else:
                    lo, hi = (mid, t - 1) if upper else (t, mid - 1)
                if hi >= lo:
                    a[rev, row, lo:hi + 1] = 1.0
        a[rev, c * (nl + 1):, :] = 1.0
    return a, lvl


def _hg_kernel(q_ref, k_ref, up_ref, dn_ref, ghi_ref, glo_ref, i_ref, a_ref, lvl_ref, o_ref, st_ref, *,
               heads, n_chunks):
    rev = pl.program_id(1)
    c, nl = HG_CHUNK, HG_LEVELS

    @pl.when(pl.program_id(2) == 0)
    def _():
        st_ref[...] = jnp.zeros_like(st_ref)

    a_mat = a_ref[0]
    lvl = lvl_ref[0]
    w = heads * LANES
    sub = lax.broadcasted_iota(jnp.int32, (SUBLANES, w), 0)
    nt = (((1,), (1,)), ((), ()))
    tn = (((0,), (0,)), ((), ()))
    head_cols = [slice(h * LANES, (h + 1) * LANES) for h in range(heads)]

    def level_operand(l, up, dn):
        m = 1 << l
        if m >= SUBLANES:
            return jnp.concatenate([(up if k & 1 else dn)[k * m:(k + 1) * m] for k in range(c // m)], axis=0)
        upper = ((sub >> l) & 1) == 1
        shape3 = (c // SUBLANES, SUBLANES, w)
        return jnp.where(upper[None], up.reshape(shape3), dn.reshape(shape3)).reshape(c, w)

    def chunk(j):
        cj = jnp.where(rev == 1, n_chunks - 1 - j, j)
        r0 = pl.multiple_of(cj * c, c)
        rows = pl.ds(r0, c)
        qb = q_ref[rows, :]
        kb = k_ref[rows, :]
        q = qb.astype(F32)
        kk = kb.astype(F32)
        up = up_ref[rows, :].astype(F32)
        dn = dn_ref[rows, :].astype(F32)
        v = i_ref[rows, :]
        e = jnp.dot(a_mat, jnp.concatenate([ghi_ref[rows, :], glo_ref[rows, :]], axis=0),
                    preferred_element_type=F32)
        b = e[0:c]
        btot = e[c * (nl + 1):c * (nl + 1) + 1]
        qd = (q * jnp.exp(b)).astype(BF16)
        khat = (kk * jnp.exp(btot - b)).astype(BF16)
        decay = jnp.exp(btot)
        xs = [(level_operand(l, up, dn) * jnp.exp(e[c * (l + 1):c * (l + 2)])).astype(BF16) for l in range(nl)]
        attns = [jnp.where(lvl == nl, lax.dot_general(qb[:, hc], kb[:, hc], nt, preferred_element_type=F32), 0.0)
                 for hc in head_cols]
        for l in range(nl):
            x = xs[l]
            attns = [jnp.where(lvl == l, lax.dot_general(x[:, hc], x[:, hc], nt, preferred_element_type=F32), at)
                     for hc, at in zip(head_cols, attns)]
        intra = [jnp.dot(at.astype(BF16), v[:, hc], preferred_element_type=F32)
                 for hc, at in zip(head_cols, attns)]
        update = [lax.dot_general(v[:, hc], khat[:, hc], tn, preferred_element_type=F32) for hc in head_cols]
        return rows, qd, decay, intra, update

    def chunk_pair(jp, carry):
        parts = [chunk(jp * HG_PAIR + u) for u in range(HG_PAIR)]
        sts = [st_ref[h] for h in range(heads)]
        for rows, qd, decay, intra, update in parts:
            for h, hc in enumerate(head_cols):
                o_ref[0, rows, hc] = intra[h] + lax.dot_general(qd[:, hc], sts[h].astype(BF16), nt,
                                                                preferred_element_type=F32)
            sts = [st * decay[:, hc] + up_h for st, hc, up_h in zip(sts, head_cols, update)]
        for h in range(heads):
            st_ref[h] = sts[h]
        return carry

    lax.fori_loop(0, n_chunks // HG_PAIR, chunk_pair, 0)


def _hgrn(hg, rest, *, batch, seq, heads, cb=1024):
    t = batch * seq
    w = heads * LANES
    nsb = seq // cb
    a_np, lvl_np = _hg_tables()
    a_np = np.concatenate([a_np, a_np], axis=-1)
    a_mat = jnp.asarray(a_np, BF16)
    lvl = jnp.asarray(lvl_np)

    def rowblk(b, d, s):
        return b * nsb + jnp.where(d == 1, nsb - 1 - s, s)

    def col(fn):
        return pl.BlockSpec((cb, w), lambda b, d, s: (rowblk(b, d, s), fn(d)))

    return pl.pallas_call(
        functools.partial(_hg_kernel, heads=heads, n_chunks=cb // HG_CHUNK),
        grid=(batch, 2, nsb),
        in_specs=[
            col(lambda d: 0),
            col(lambda d: 1 + d),
            col(lambda d: 2 * d),
            col(lambda d: 1 - d),
            col(lambda d: 3 + 2 * d),
            col(lambda d: 4 + 2 * d),
            col(lambda d: 0),
            pl.BlockSpec((1,) + a_np.shape[1:], lambda b, d, s: (d, 0, 0)),
            pl.BlockSpec((1,) + lvl_np.shape[1:], lambda b, d, s: (d, 0, 0)),
        ],
        out_specs=pl.BlockSpec((1, cb, w), lambda b, d, s: (d, rowblk(b, d, s), 0)),
        out_shape=jax.ShapeDtypeStruct((2, t, w), F32),
        scratch_shapes=[pltpu.VMEM((heads, LANES, LANES), F32)],
        compiler_params=pltpu.CompilerParams(
            dimension_semantics=("parallel", "parallel", "arbitrary"), vmem_limit_bytes=VMEM_LIMIT),
        name="hgrn2_scan",
    )(hg, hg, hg, hg, hg, hg, rest, a_mat, lvl)


def _first_index_of_max(x, idx):
    m = jnp.max(x, axis=0, keepdims=True)
    i = jnp.min(jnp.where(x == m, idx, EXPERTS_PER_GROUP), axis=0, keepdims=True)
    return m, i


def _merge_kernel(a_ref, o2_ref, g_ref, ga_ref, gb_ref, h0_ref, ng_ref,
                  wa_ref, wb_ref, wo_ref, ln1g_ref, ln1b_ref, wrh_ref, wrl_ref, br_ref,
                  h1_ref, ri_ref, rw_ref, xs_ref, pc_ref, *, heads, alpha):
    o = o2_ref[0] + o2_ref[1]
    gate = g_ref[...].astype(F32)
    parts = []
    for h in range(heads):
        cols = slice(h * LANES, (h + 1) * LANES)
        oh = o[:, cols]
        ms = jnp.mean(oh * oh, axis=-1, keepdims=True)
        gh = gate[:, cols]
        parts.append((oh * lax.rsqrt(ms + RMS_EPS) * ng_ref[:, cols] * (gh * _sigmoid(gh))).astype(BF16))
    c = jnp.concatenate(parts, axis=-1)
    pa = jnp.dot(a_ref[...], wa_ref[...], preferred_element_type=F32)
    pb = jnp.dot(c, wb_ref[...], preferred_element_type=F32)
    merged = _sigmoid(ga_ref[...].astype(F32)) * pa + _sigmoid(gb_ref[...].astype(F32)) * pb
    y = jnp.dot(merged.astype(BF16), wo_ref[...], preferred_element_type=F32)
    h1 = _ln(alpha * h0_ref[...] + y, ln1g_ref[...], ln1b_ref[...])
    h1_ref[...] = h1

    nt = (((1,), (1,)), ((), ()))
    h_hi = h1.astype(BF16)
    h_lo = (h1 - h_hi.astype(F32)).astype(BF16)
    logits = (lax.dot_general(wrh_ref[...], h_hi, nt, preferred_element_type=F32)
              + lax.dot_general(wrh_ref[...], h_lo, nt, preferred_element_type=F32)
              + lax.dot_general(wrl_ref[...], h_hi, nt, preferred_element_type=F32) + br_ref[:, 0:1])
    n = logits.shape[1]
    idx = lax.broadcasted_iota(jnp.int32, (EXPERTS_PER_GROUP, n), 0)
    glog = jnp.where(idx < N_GROUPS, logits[0:EXPERTS_PER_GROUP], -jnp.inf)
    gmax, gsel = _first_index_of_max(glog, idx)
    gw = 1.0 / jnp.sum(jnp.exp(glog - gmax), axis=0, keepdims=True)
    elog = logits[EXPERTS_PER_GROUP:2 * EXPERTS_PER_GROUP]
    for gi in range(1, N_GROUPS):
        lo = EXPERTS_PER_GROUP * (gi + 1)
        elog = jnp.where(gsel == gi, logits[lo:lo + EXPERTS_PER_GROUP], elog)
    ee = jnp.exp(elog - jnp.max(elog, axis=0, keepdims=True))
    p = ee / jnp.sum(ee, axis=0, keepdims=True)
    p1, i1 = _first_index_of_max(p, idx)
    p2, i2 = _first_index_of_max(jnp.where(idx == i1, -1.0, p), idx)
    denom = p1 + p2
    e1 = gsel * EXPERTS_PER_GROUP + i1
    e2 = gsel * EXPERTS_PER_GROUP + i2
    rw_ref[...] = jnp.where(idx == 0, gw * p1 / denom, jnp.where(idx == 1, gw * p2 / denom, 0.0))

    ex = lax.broadcasted_iota(jnp.int32, (N_EXPERTS, n), 0)
    m1 = ex == e1
    m2 = ex == e2
    member = jnp.where(m1, 1.0, jnp.where(m2, 1.0, 0.0))
    before = (lax.broadcasted_iota(jnp.int32, (n, n), 0) < lax.broadcasted_iota(jnp.int32, (n, n), 1))
    rank = jnp.dot(member.astype(BF16), jnp.where(before, 1.0, 0.0).astype(BF16), preferred_element_type=F32)
    count = jnp.sum(member, axis=1, keepdims=True)
    chunks = jnp.floor((count + (MOE_CHUNK - 1)) * (1.0 / MOE_CHUNK))
    chunks_b = jnp.broadcast_to(chunks, (N_EXPERTS, LANES))
    lower = (lax.broadcasted_iota(jnp.int32, (N_EXPERTS, N_EXPERTS), 1)
             < lax.broadcasted_iota(jnp.int32, (N_EXPERTS, N_EXPERTS), 0))
    seg = jnp.dot(jnp.where(lower, 1.0, 0.0), chunks_b, precision=lax.Precision.HIGHEST,
                  preferred_element_type=F32)
    posmat = seg[:, 0:1] * MOE_CHUNK + rank
    pos1 = jnp.sum(jnp.where(m1, posmat, 0.0), axis=0, keepdims=True).astype(jnp.int32)
    pos2 = jnp.sum(jnp.where(m2, posmat, 0.0), axis=0, keepdims=True).astype(jnp.int32)
    ri_ref[...] = jnp.where(idx == 0, pos1, jnp.where(idx == 1, pos2, 0))
    pc_ref[...] = chunks_b
    slot = lax.broadcasted_iota(jnp.int32, (xs_ref.shape[0], n), 0)
    perm = jnp.where(slot == pos1, 1.0, jnp.where(slot == pos2, 1.0, 0.0)).astype(BF16)
    xs_ref[...] = jnp.dot(perm, h_hi, preferred_element_type=F32)


def _merge(a, o2, rest, h0, norm_g, wa, wb, wo, ln1g, ln1b, wr_t, br_col, *, heads, alpha):
    t, d = h0.shape
    w = heads * LANES
    wa_w = a.shape[1]
    tm = MOE_TM
    n_tiles = t // tm
    wr_hi = wr_t.astype(BF16)
    wr_lo = (wr_t - wr_hi.astype(F32)).astype(BF16)
    full = lambda shape: pl.BlockSpec(shape, lambda i: (0,) * len(shape))
    return pl.pallas_call(
        functools.partial(_merge_kernel, heads=heads, alpha=alpha),
        grid=(t // tm,),
        in_specs=[
            pl.BlockSpec((tm, wa_w), lambda i: (i, 0)),
            pl.BlockSpec((2, tm, w), lambda i: (0, i, 0)),
            pl.BlockSpec((tm, w), lambda i: (i, 1)),
            pl.BlockSpec((tm, d), lambda i: (i, 1)),
            pl.BlockSpec((tm, d), lambda i: (i, 2)),
            pl.BlockSpec((tm, d), lambda i: (i, 0)),
            full((1, w)),
            full(wa.shape), full(wb.shape), full(wo.shape),
            full((1, d)), full((1, d)),
            full(wr_t.shape), full(wr_t.shape), full(br_col.shape),
        ],
        out_specs=[
            pl.BlockSpec((tm, d), lambda i: (i, 0)),
            pl.BlockSpec((EXPERTS_PER_GROUP, tm), lambda i: (0, i)),
            pl.BlockSpec((EXPERTS_PER_GROUP, tm), lambda i: (0, i)),
            pl.BlockSpec((MOE_CAP, d), lambda i: (i, 0)),
            pl.BlockSpec((N_EXPERTS, LANES), lambda i: (i, 0)),
        ],
        out_shape=[
            jax.ShapeDtypeStruct((t, d), F32),
            jax.ShapeDtypeStruct((EXPERTS_PER_GROUP, t), jnp.int32),
            jax.ShapeDtypeStruct((EXPERTS_PER_GROUP, t), F32),
            jax.ShapeDtypeStruct((n_tiles * MOE_CAP, d), F32),
            jax.ShapeDtypeStruct((n_tiles * N_EXPERTS, LANES), F32),
        ],
        compiler_params=pltpu.CompilerParams(
            dimension_semantics=("parallel",), vmem_limit_bytes=VMEM_LIMIT),
        name="merge_outproj_router",
    )(a, o2, rest, rest, rest, h0, norm_g.reshape(1, w),
      wa, wb, wo, ln1g.reshape(1, d), ln1b.reshape(1, d), wr_hi, wr_lo, br_col)


def _expert_kernel(be_ref, nv_ref, cur_ref, nxt_ref, xy_in, wg_ref, wu_ref, wd_ref, xy_out,
                   xbuf, ybuf, wg_s, wu_s, wd_s, gsem, ssem):
    i = pl.program_id(0)
    n = pl.num_programs(0)
    slot = i % 2
    nv = nv_ref[i]

    def gather_copy(tbl_ref, s, buf_slot):
        src = pl.multiple_of(tbl_ref[0, 0, s] * MOE_CHUNK, MOE_CHUNK)
        dst = pl.multiple_of(s * MOE_CHUNK, MOE_CHUNK)
        return pltpu.make_async_copy(xy_in.at[pl.ds(src, MOE_CHUNK), :],
                                     xbuf.at[buf_slot, pl.ds(dst, MOE_CHUNK), :], gsem.at[buf_slot])

    def scatter_copy(s, buf_slot):
        src = pl.multiple_of(s * MOE_CHUNK, MOE_CHUNK)
        dst = pl.multiple_of(cur_ref[0, 0, s] * MOE_CHUNK, MOE_CHUNK)
        return pltpu.make_async_copy(ybuf.at[buf_slot, pl.ds(src, MOE_CHUNK), :],
                                     xy_out.at[pl.ds(dst, MOE_CHUNK), :], ssem.at[buf_slot])

    def for_each_chunk(count, fn):
        for g0 in range(0, MOE_CPB, MOE_DMA_GROUP):
            @pl.when(count >= g0 + MOE_DMA_GROUP)
            def _():
                for s in range(g0, g0 + MOE_DMA_GROUP):
                    fn(s)

        def body(s, c):
            fn(s)
            return c
        lax.fori_loop(count // MOE_DMA_GROUP * MOE_DMA_GROUP, count, body, 0)

    def start_gathers(tbl_ref, buf_slot):
        for s in range(MOE_CPB):
            gather_copy(tbl_ref, s, buf_slot).start()

    def wait_scatters(count, buf_slot):
        for_each_chunk(count, lambda s: scatter_copy(0, buf_slot).wait())

    @pl.when((i == 0) & (nv > 0))
    def _():
        start_gathers(cur_ref, slot)

    nxt = jnp.minimum(i + 1, n - 1)

    @pl.when((i + 1 < n) & (nv_ref[nxt] > 0))
    def _():
        start_gathers(nxt_ref, 1 - slot)

    @pl.when(i >= 2)
    def _():
        wait_scatters(nv_ref[jnp.maximum(i - 2, 0)], slot)

    @pl.when(nv > 0)
    def _():
        for _ in range(MOE_CPB):
            gather_copy(cur_ref, 0, slot).wait()

        @pl.when((i == 0) | (be_ref[i] != be_ref[jnp.maximum(i - 1, 0)]))
        def _():
            wg_s[...] = wg_ref[0].astype(BF16)
            wu_s[...] = wu_ref[0].astype(BF16)
            wd_s[...] = wd_ref[0].astype(BF16)

        x = xbuf[slot].astype(BF16)
        gate = jnp.dot(x, wg_s[...], preferred_element_type=F32)
        up = jnp.dot(x, wu_s[...], preferred_element_type=F32)
        hid = (gate * _sigmoid(gate) * up).astype(BF16)
        ybuf[slot] = jnp.dot(hid, wd_s[...], preferred_element_type=F32)

        for_each_chunk(nv, lambda s: scatter_copy(s, slot).start())

    @pl.when(i == n - 1)
    def _():
        @pl.when(i >= 1)
        def _():
            wait_scatters(nv_ref[jnp.maximum(i - 1, 0)], 1 - slot)
        wait_scatters(nv, slot)


def _experts(xy, block_expert, block_nvalid, table, wg, wu, wd):
    rows, d = xy.shape
    n_blocks = table.shape[0]
    ne, _, dff = wg.shape
    return pl.pallas_call(
        _expert_kernel,
        grid_spec=pltpu.PrefetchScalarGridSpec(
            num_scalar_prefetch=2,
            grid=(n_blocks,),
            in_specs=[
                pl.BlockSpec((1, 1, MOE_CPB), lambda i, be, nv: (i, 0, 0), memory_space=pltpu.SMEM),
                pl.BlockSpec((1, 1, MOE_CPB), lambda i, be, nv: (jnp.minimum(i + 1, n_blocks - 1), 0, 0),
                             memory_space=pltpu.SMEM),
                pl.BlockSpec(memory_space=pl.ANY),
                pl.BlockSpec((1, d, dff), lambda i, be, nv: (be[i], 0, 0)),
                pl.BlockSpec((1, d, dff), lambda i, be, nv: (be[i], 0, 0)),
                pl.BlockSpec((1, dff, d), lambda i, be, nv: (be[i], 0, 0)),
            ],
            out_specs=pl.BlockSpec(memory_space=pl.ANY),
            scratch_shapes=[
                pltpu.VMEM((2, MOE_BM, d), F32),
                pltpu.VMEM((2, MOE_BM, d), F32),
                pltpu.VMEM((d, dff), BF16),
                pltpu.VMEM((d, dff), BF16),
                pltpu.VMEM((dff, d), BF16),
                pltpu.SemaphoreType.DMA((2,)),
                pltpu.SemaphoreType.DMA((2,)),
            ],
        ),
        out_shape=jax.ShapeDtypeStruct((rows, d), F32),
        input_output_aliases={4: 0},
        compiler_params=pltpu.CompilerParams(
            dimension_semantics=("arbitrary",), vmem_limit_bytes=VMEM_LIMIT),
        name="expert_ffn",
    )(block_expert, block_nvalid, table, table, xy, wg, wu, wd)


def _chunk_plan(tile_chunks):
    n_tiles = tile_chunks.shape[0]
    cap_chunks = MOE_CAP // MOE_CHUNK
    max_tile_chunks = (TOP_K * MOE_TM + N_EXPERTS * (MOE_CHUNK - 1)) // MOE_CHUNK
    n_blocks = -(-n_tiles * max_tile_chunks // MOE_CPB) + N_EXPERTS
    pc = tile_chunks.astype(jnp.int32)
    seg = jnp.cumsum(pc, axis=1) - pc
    tile_end = jnp.cumsum(pc.T, axis=1)
    tile_beg = tile_end - pc.T
    tot = tile_end[:, -1]
    nblk = (tot + MOE_CPB - 1) // MOE_CPB
    bend = jnp.cumsum(nblk)
    bbeg = bend - nblk
    bidx = jnp.arange(n_blocks, dtype=jnp.int32)
    be = jnp.minimum(jnp.sum((bend[None, :] <= bidx[:, None]).astype(jnp.int32), axis=1), N_EXPERTS - 1)
    onehot_e = (be[:, None] == jnp.arange(N_EXPERTS, dtype=jnp.int32)[None, :]).astype(jnp.int32)
    pick = lambda vec: jnp.sum(onehot_e * vec[None, :], axis=1)
    q = ((bidx - pick(bbeg)) * MOE_CPB)[:, None] + jnp.arange(MOE_CPB, dtype=jnp.int32)[None, :]
    valid = (q < pick(tot)[:, None]) & (bidx < bend[-1])[:, None]
    rows_of = lambda tbl: jnp.sum(onehot_e[:, :, None] * tbl[None, :, :], axis=1)
    blk_end = rows_of(tile_end)
    base = jnp.arange(n_tiles, dtype=jnp.int32)[None, :] * cap_chunks + seg.T - tile_beg
    blk_base = rows_of(base)
    tile = jnp.sum((blk_end[:, None, :] <= q[:, :, None]).astype(jnp.int32), axis=2)
    tile = jnp.minimum(tile, n_tiles - 1)
    onehot_t = tile[:, :, None] == jnp.arange(n_tiles, dtype=jnp.int32)[None, None, :]
    src = q + jnp.sum(jnp.where(onehot_t, blk_base[:, None, :], 0), axis=2)
    src = jnp.where(valid, src, src[:, 0:1])
    nvalid = jnp.sum(valid, axis=1).astype(jnp.int32)
    src = jnp.where(nvalid[:, None] > 0, src, 0).astype(jnp.int32)
    return be.astype(jnp.int32), nvalid, src.reshape(n_blocks, 1, MOE_CPB)


def _final_kernel(h1_ref, ys_ref, pos_ref, w_ref, g_ref, b_ref, o_ref, *, alpha):
    cap = ys_ref.shape[0]
    n = h1_ref.shape[0]
    slot = lax.broadcasted_iota(jnp.int32, (cap, n), 0)
    pw = (jnp.where(slot == pos_ref[0:1, :], w_ref[0:1, :], 0.0)
          + jnp.where(slot == pos_ref[1:2, :], w_ref[1:2, :], 0.0))
    pw_hi = pw.astype(BF16)
    pw_lo = (pw - pw_hi.astype(F32)).astype(BF16)
    ys = ys_ref[...].astype(BF16)
    tn = (((0,), (0,)), ((), ()))
    moe = (lax.dot_general(pw_hi, ys, tn, preferred_element_type=F32)
           + lax.dot_general(pw_lo, ys, tn, preferred_element_type=F32))
    o_ref[...] = _ln(alpha * h1_ref[...] + moe, g_ref[...], b_ref[...])


def _final(h1, ys, pos, wts, g, b, *, alpha):
    t, d = h1.shape
    tm = MOE_TM
    return pl.pallas_call(
        functools.partial(_final_kernel, alpha=alpha),
        grid=(t // tm,),
        in_specs=[
            pl.BlockSpec((tm, d), lambda i: (i, 0)),
            pl.BlockSpec((MOE_CAP, d), lambda i: (i, 0)),
            pl.BlockSpec((EXPERTS_PER_GROUP, tm), lambda i: (0, i)),
            pl.BlockSpec((EXPERTS_PER_GROUP, tm), lambda i: (0, i)),
            pl.BlockSpec((1, d), lambda i: (0, 0)),
            pl.BlockSpec((1, d), lambda i: (0, 0)),
        ],
        out_specs=pl.BlockSpec((tm, d), lambda i: (i, 0)),
        out_shape=jax.ShapeDtypeStruct((t, d), F32),
        compiler_params=pltpu.CompilerParams(
            dimension_semantics=("parallel",), vmem_limit_bytes=VMEM_LIMIT),
        name="combine_postnorm",
    )(h1, ys, pos, wts, g.reshape(1, d), b.reshape(1, d))


def kernel(x, emb_ln_g, emb_ln_b, w_in, na_rpb, hg_lb, hg_norm_g, w_proj_a, w_proj_b, w_out, ln1_g, ln1_b,
           w_router_group, b_router_group, w_router_expert, b_router_expert, w_gate, w_up, w_down, ln2_g, ln2_b):
    batch, seq, d = x.shape
    depth = w_in.shape[0]
    t = batch * seq
    na_w = na_rpb.shape[1] * NA_HEAD_DIM
    hg_w = hg_norm_g.shape[1]
    hg_heads = hg_w // HG_DK
    alpha = float((2 * depth) ** 0.25)
    lb_all = jnp.cumsum(jax.nn.softmax(hg_lb.astype(F32), axis=1), axis=1)

    assert depth == 1, depth
    l = 0
    x2d = x.reshape(t, d)
    h0, na, hg, rest = _inproj(x2d, emb_ln_g, emb_ln_b, w_in[l].astype(BF16), lb_all[:, l], n_na=3 * na_w, hg_w=hg_w,
                           q_cols=na_w, q_scale=float(NA_HEAD_DIM ** -0.5 * LOG2_E))
    bias = _na_bias_tables(na_rpb[l], seq // GRID_W)
    a = _na(na.reshape(batch, seq, 3 * na_w), bias, batch=batch, seq=seq).reshape(t, na_w)
    o2 = _hgrn(hg, rest, batch=batch, seq=seq, heads=hg_heads)

    wr_t = jnp.zeros((LANES, d), F32)
    wr_t = wr_t.at[0:N_GROUPS].set(w_router_group[l].T.astype(F32))
    wr_t = wr_t.at[EXPERTS_PER_GROUP:EXPERTS_PER_GROUP + N_EXPERTS].set(w_router_expert[l].T.astype(F32))
    br = jnp.zeros((LANES,), F32)
    br = br.at[0:N_GROUPS].set(b_router_group[l].astype(F32))
    br = br.at[EXPERTS_PER_GROUP:EXPERTS_PER_GROUP + N_EXPERTS].set(b_router_expert[l].astype(F32))
    br_col = jnp.broadcast_to(br[:, None], (LANES, LANES))

    h1, pos, route_w, xs, tile_chunks = _merge(
        a, o2, rest, h0, hg_norm_g[l], w_proj_a[l].astype(BF16), w_proj_b[l].astype(BF16),
        w_out[l].astype(BF16), ln1_g[l], ln1_b[l], wr_t, br_col, heads=hg_heads, alpha=alpha)

    plan = _chunk_plan(tile_chunks[:, 0].reshape(t // MOE_TM, N_EXPERTS))
    ys = _experts(xs, *plan, w_gate[l], w_up[l], w_down[l])
    h2 = _final(h1, ys, pos, route_w, ln2_g[l], ln2_b[l], alpha=alpha)
    return h2.reshape(batch, seq, d)
```

```python
import functools

import numpy as np
import jax
import jax.numpy as jnp
from jax import lax
from jax.experimental import pallas as pl
from jax.experimental.pallas import tpu as pltpu

F32 = jnp.float32
BF16 = jnp.bfloat16

GRID_W = 64
NA_HEAD_DIM = 64
NA_KH = 8
NA_KW = 16
HG_DK = 128
N_GROUPS = 4
EXPERTS_PER_GROUP = 8
N_EXPERTS = N_GROUPS * EXPERTS_PER_GROUP
TOP_K = 2
LN_EPS = 1e-5
RMS_EPS = 1e-6

LANES = 128
SUBLANES = 8
VMEM_LIMIT = 56 * 1024 * 1024
NEG_BIG = -1e30
LOG2_E = 1.4426950408889634

NA_QROWS = 4
NA_KROWS = NA_QROWS + NA_KH
HG_CHUNK = 128
HG_LEVELS = 7
HG_PAIR = 2
MOE_TM = 512
MOE_CHUNK = 8
MOE_CAP = TOP_K * MOE_TM + N_EXPERTS * MOE_CHUNK
MOE_BM = 256
MOE_CPB = MOE_BM // MOE_CHUNK
MOE_DMA_GROUP = 8


def _ln(x, g, b):
    mu = jnp.mean(x, axis=-1, keepdims=True)
    xc = x - mu
    var = jnp.mean(xc * xc, axis=-1, keepdims=True)
    return xc * lax.rsqrt(var + LN_EPS) * g + b


def _sigmoid(x):
    return 0.5 * jnp.tanh(0.5 * x) + 0.5


def _inproj_kernel(x_ref, g_ref, b_ref, w_ref, lb_ref, h0_ref, na_ref, hg_ref, rest_ref, *,
                   n_na, hg_w, q_cols, q_scale):
    h0 = _ln(x_ref[...], g_ref[...], b_ref[...])
    h0_ref[...] = h0
    h = h0.astype(BF16)
    d_in = w_ref.shape[1]
    cn = hg_w
    for c0 in range(0, d_in, cn):
        u = jnp.dot(h, w_ref[:, c0:c0 + cn], preferred_element_type=F32)
        if c0 < q_cols:
            u = u * q_scale
        if c0 < n_na:
            na_ref[:, c0:c0 + cn] = u.astype(na_ref.dtype)
        elif c0 == n_na:
            hg_ref[:, 0:cn] = (u * _sigmoid(u)).astype(hg_ref.dtype)
        elif c0 < n_na + 3 * hg_w:
            d = (c0 - n_na) // hg_w - 1
            lb = lb_ref[d]
            forget = lb + (1.0 - lb) * _sigmoid(u)
            logf = jnp.log(forget)
            hi = logf.astype(BF16)
            hg_ref[:, (1 + d) * cn:(2 + d) * cn] = (1.0 - forget).astype(hg_ref.dtype)
            hg_ref[:, (3 + 2 * d) * cn:(4 + 2 * d) * cn] = hi
            hg_ref[:, (4 + 2 * d) * cn:(5 + 2 * d) * cn] = (logf - hi.astype(F32)).astype(BF16)
        else:
            o = c0 - n_na - 3 * hg_w
            rest_ref[:, o:o + cn] = u.astype(rest_ref.dtype)


def _inproj(x2d, g, b, w_bf16, lb, *, n_na, hg_w, q_cols, q_scale, tm=512):
    t, d = x2d.shape
    d_in = w_bf16.shape[1]
    n_rest = d_in - n_na - 3 * hg_w
    kern = functools.partial(_inproj_kernel, n_na=n_na, hg_w=hg_w, q_cols=q_cols, q_scale=q_scale)
    return pl.pallas_call(
        kern,
        grid=(t // tm,),
        in_specs=[
            pl.BlockSpec((tm, d), lambda i: (i, 0)),
            pl.BlockSpec((1, d), lambda i: (0, 0)),
            pl.BlockSpec((1, d), lambda i: (0, 0)),
            pl.BlockSpec((d, d_in), lambda i: (0, 0), pipeline_mode=pl.Buffered(1)),
            pl.BlockSpec((2, 1, hg_w), lambda i: (0, 0, 0)),
        ],
        out_specs=[
            pl.BlockSpec((tm, d), lambda i: (i, 0)),
            pl.BlockSpec((tm, n_na), lambda i: (i, 0)),
            pl.BlockSpec((tm, 7 * hg_w), lambda i: (i, 0)),
            pl.BlockSpec((tm, n_rest), lambda i: (i, 0)),
        ],
        out_shape=[
            jax.ShapeDtypeStruct((t, d), F32),
            jax.ShapeDtypeStruct((t, n_na), BF16),
            jax.ShapeDtypeStruct((t, 7 * hg_w), BF16),
            jax.ShapeDtypeStruct((t, n_rest), BF16),
        ],
        compiler_params=pltpu.CompilerParams(
            dimension_semantics=("parallel",), vmem_limit_bytes=VMEM_LIMIT),
        name="ln_inproj",
    )(x2d, g.reshape(1, d), b.reshape(1, d), w_bf16, lb.reshape(2, 1, hg_w))


def _na_bias_tables(rpb, rows):
    nq = NA_QROWS * GRID_W
    nk = NA_KROWS * GRID_W
    qi = np.arange(nq)
    kj = np.arange(nk)
    qc = (qi % GRID_W)[:, None]
    kc = (kj % GRID_W)[None, :]
    cs = np.clip(qc - NA_KW // 2, 0, GRID_W - NA_KW)
    valid_c = (kc >= cs) & (kc < cs + NA_KW)
    n_rel_r, n_rel_c = 2 * NA_KH - 1, 2 * NA_KW - 1
    wc = np.arange(GRID_W)
    cidx = np.clip(wc[None, :] - wc[:, None] + NA_KW - 1, 0, n_rel_c - 1)
    onehot_c = (cidx[None] == np.arange(n_rel_c)[:, None, None]).astype(np.float32)
    ridx_all, valid_all = [], []
    for r0 in (0, NA_QROWS, rows - NA_QROWS):
        start = int(np.clip(r0 - NA_KH // 2, 0, rows - NA_KROWS))
        qr = (r0 + qi // GRID_W)[:, None]
        kr = (start + kj // GRID_W)[None, :]
        rs = np.clip(qr - NA_KH // 2, 0, rows - NA_KH)
        valid_all.append((kr >= rs) & (kr < rs + NA_KH) & valid_c)
        qr_s = r0 + np.arange(NA_QROWS)[:, None]
        kr_s = start + np.arange(NA_KROWS)[None, :]
        ridx_all.append(np.clip(kr_s - qr_s + NA_KH - 1, 0, n_rel_r - 1))
    valid = np.stack(valid_all)
    rows_sel = rpb.astype(F32)[:, np.stack(ridx_all), :]
    bias = jnp.einsum('hcqkj,jxy->chqxky', rows_sel, onehot_c, precision=lax.Precision.HIGHEST)
    bias = bias.reshape(3, rpb.shape[0], nq, nk) * LOG2_E
    return jnp.where(valid[:, None], bias, NEG_BIG)


def _na_kernel(q_ref, k_ref, v_ref, bias_ref, o_ref, *, rows):
    nq = NA_QROWS * GRID_W
    nk = NA_KROWS * GRID_W
    nrb = rows // NA_QROWS
    half = nq // 2
    lane = lax.broadcasted_iota(jnp.int32, (1, LANES), 1)
    first = lane < NA_HEAD_DIM
    work = [(hh, r0) for hh in range(2) for r0 in (0, half)]

    def row_block(rb, carry):
        start = jnp.clip(rb * NA_QROWS - NA_KH // 2, 0, rows - NA_KROWS)
        kstart = pl.multiple_of(start * GRID_W, NA_QROWS * GRID_W)
        qstart = pl.multiple_of(rb * nq, nq)
        cls = jnp.where(rb == 0, 0, jnp.where(rb == nrb - 1, 2, 1))
        kwin = k_ref[0, pl.ds(kstart, nk), :]
        vwin = v_ref[0, pl.ds(kstart, nk), :]
        q = q_ref[0, pl.ds(qstart, nq), :]
        zero = jnp.zeros_like(q)
        qs = [jnp.where(first, q, zero), jnp.where(first, zero, q)]
        ss = [lax.dot_general(qs[hh][r0:r0 + half], kwin, (((1,), (1,)), ((), ())), preferred_element_type=F32)
              + bias_ref[cls, 0, hh, r0:r0 + half, :] for hh, r0 in work]
        ps = [jnp.exp2(s - jnp.max(s, axis=-1, keepdims=True)) for s in ss]
        ls = [jnp.sum(p, axis=-1, keepdims=True) for p in ps]
        os = [jnp.dot(p.astype(BF16), vwin, preferred_element_type=F32) / l for p, l in zip(ps, ls)]
        o_ref[0, pl.ds(qstart, nq), :] = jnp.where(first, jnp.concatenate(os[0:2], axis=0),
                                                   jnp.concatenate(os[2:4], axis=0)).astype(o_ref.dtype)
        return carry

    lax.fori_loop(0, nrb, row_block, 0)


def _na(qkv, bias, *, batch, seq):
    rows = seq // GRID_W
    width = qkv.shape[-1] // 3
    pairs = width // LANES
    nq = NA_QROWS * GRID_W
    nk = NA_KROWS * GRID_W
    bias5 = bias.reshape(3, pairs, 2, nq, nk)
    return pl.pallas_call(
        functools.partial(_na_kernel, rows=rows),
        grid=(pairs, batch),
        in_specs=[
            pl.BlockSpec((1, seq, LANES), lambda p, b: (b, 0, p)),
            pl.BlockSpec((1, seq, LANES), lambda p, b: (b, 0, pairs + p)),
            pl.BlockSpec((1, seq, LANES), lambda p, b: (b, 0, 2 * pairs + p)),
            pl.BlockSpec((3, 1, 2, nq, nk), lambda p, b: (0, p, 0, 0, 0)),
        ],
        out_specs=pl.BlockSpec((1, seq, LANES), lambda p, b: (b, 0, p)),
        out_shape=jax.ShapeDtypeStruct((batch, seq, width), BF16),
        compiler_params=pltpu.CompilerParams(
            dimension_semantics=("parallel", "parallel"), vmem_limit_bytes=VMEM_LIMIT),
        name="neighbourhood_attention",
    )(qkv, qkv, qkv, bias5)


def _hg_tables():
    c, nl = HG_CHUNK, HG_LEVELS
    a = np.zeros((2, c * (nl + 1) + 8, c), np.float32)
    lvl = np.full((2, c, c), -1, np.int32)
    for rev in (0, 1):
        for t in range(c):
            if rev == 0:
                a[rev, t, :t + 1] = 1.0
            else:
                a[rev, t, t:] = 1.0
            lvl[rev, t, t] = nl
            for s in range(c):
                if (s < t and rev == 0) or (s > t and rev == 1):
                    lvl[rev, t, s] = int(np.floor(np.log2(t ^ s)))
            for l in range(nl):
                m = 1 << l
                mid = (t & ~(2 * m - 1)) + m
                upper = (t >> l) & 1
                row = c * (l + 1) + t
                if rev == 0:
                    lo, hi = (mid, t) if upper else (t + 1, mid - 1)
                else:
                    lo, hi = (mid, t - 1) if upper else (t, mid - 1)
                if hi >= lo:
                    a[rev, row, lo:hi + 1] = 1.0
        a[rev, c * (nl + 1):, :] = 1.0
    return a, lvl


def _hg_kernel(q_ref, k_ref, up_ref, dn_ref, ghi_ref, glo_ref, i_ref, a_ref, lvl_ref, o_ref, st_ref, *,
               heads, n_chunks):
    rev = pl.program_id(1)
    c, nl = HG_CHUNK, HG_LEVELS

    @pl.when(pl.program_id(2) == 0)
    def _():
        st_ref[...] = jnp.zeros_like(st_ref)

    a_mat = a_ref[0]
    lvl = lvl_ref[0]
    w = heads * LANES
    sub = lax.broadcasted_iota(jnp.int32, (SUBLANES, w), 0)
    nt = (((1,), (1,)), ((), ()))
    tn = (((0,), (0,)), ((), ()))
    head_cols = [slice(h * LANES, (h + 1) * LANES) for h in range(heads)]

    def level_operand(l, up, dn):
        m = 1 << l
        if m >= SUBLANES:
            return jnp.concatenate([(up if k & 1 else dn)[k * m:(k + 1) * m] for k in range(c // m)], axis=0)
        upper = ((sub >> l) & 1) == 1
        shape3 = (c // SUBLANES, SUBLANES, w)
        return jnp.where(upper[None], up.reshape(shape3), dn.reshape(shape3)).reshape(c, w)

    def chunk(j):
        cj = jnp.where(rev == 1, n_chunks - 1 - j, j)
        r0 = pl.multiple_of(cj * c, c)
        rows = pl.ds(r0, c)
        qb = q_ref[rows, :]
        kb = k_ref[rows, :]
        q = qb.astype(F32)
        kk = kb.astype(F32)
        up = up_ref[rows, :].astype(F32)
        dn = dn_ref[rows, :].astype(F32)
        v = i_ref[rows, :]
        e = jnp.dot(a_mat, jnp.concatenate([ghi_ref[rows, :], glo_ref[rows, :]], axis=0),
                    preferred_element_type=F32)
        b = e[0:c]
        btot = e[c * (nl + 1):c * (nl + 1) + 1]
        qd = (q * jnp.exp(b)).astype(BF16)
        khat = (kk * jnp.exp(btot - b)).astype(BF16)
        decay = jnp.exp(btot)
        xs = [(level_operand(l, up, dn) * jnp.exp(e[c * (l + 1):c * (l + 2)])).astype(BF16) for l in range(nl)]
        attns = [jnp.where(lvl == nl, lax.dot_general(qb[:, hc], kb[:, hc], nt, preferred_element_type=F32), 0.0)
                 for hc in head_cols]
        for l in range(nl):
            x = xs[l]
            attns = [jnp.where(lvl == l, lax.dot_general(x[:, hc], x[:, hc], nt, preferred_element_type=F32), at)
                     for hc, at in zip(head_cols, attns)]
        intra = [jnp.dot(at.astype(BF16), v[:, hc], preferred_element_type=F32)
                 for hc, at in zip(head_cols, attns)]
        update = [lax.dot_general(v[:, hc], khat[:, hc], tn, preferred_element_type=F32) for hc in head_cols]
        return rows, qd, decay, intra, update

    def chunk_pair(jp, carry):
        parts = [chunk(jp * HG_PAIR + u) for u in range(HG_PAIR)]
        sts = [st_ref[h] for h in range(heads)]
        for rows, qd, decay, intra, update in parts:
            for h, hc in enumerate(head_cols):
                o_ref[0, rows, hc] = (intra[h] + lax.dot_general(qd[:, hc], sts[h].astype(BF16), nt,
                                                                 preferred_element_type=F32)).astype(o_ref.dtype)
            sts = [st * decay[:, hc] + up_h for st, hc, up_h in zip(sts, head_cols, update)]
        for h in range(heads):
            st_ref[h] = sts[h]
        return carry

    lax.fori_loop(0, n_chunks // HG_PAIR, chunk_pair, 0)


def _hgrn(hg, rest, *, batch, seq, heads, cb=1024):
    t = batch * seq
    w = heads * LANES
    nsb = seq // cb
    a_np, lvl_np = _hg_tables()
    a_np = np.concatenate([a_np, a_np], axis=-1)
    a_mat = jnp.asarray(a_np, BF16)
    lvl = jnp.asarray(lvl_np)

    def rowblk(b, d, s):
        return b * nsb + jnp.where(d == 1, nsb - 1 - s, s)

    def col(fn):
        return pl.BlockSpec((cb, w), lambda b, d, s: (rowblk(b, d, s), fn(d)))

    return pl.pallas_call(
        functools.partial(_hg_kernel, heads=heads, n_chunks=cb // HG_CHUNK),
        grid=(batch, 2, nsb),
        in_specs=[
            col(lambda d: 0),
            col(lambda d: 1 + d),
            col(lambda d: 2 * d),
            col(lambda d: 1 - d),
            col(lambda d: 3 + 2 * d),
            col(lambda d: 4 + 2 * d),
            col(lambda d: 0),
            pl.BlockSpec((1,) + a_np.shape[1:], lambda b, d, s: (d, 0, 0)),
            pl.BlockSpec((1,) + lvl_np.shape[1:], lambda b, d, s: (d, 0, 0)),
        ],
        out_specs=pl.BlockSpec((1, cb, w), lambda b, d, s: (d, rowblk(b, d, s), 0)),
        out_shape=jax.ShapeDtypeStruct((2, t, w), BF16),
        scratch_shapes=[pltpu.VMEM((heads, LANES, LANES), F32)],
        compiler_params=pltpu.CompilerParams(
            dimension_semantics=("parallel", "parallel", "arbitrary"), vmem_limit_bytes=VMEM_LIMIT),
        name="hgrn2_scan",
    )(hg, hg, hg, hg, hg, hg, rest, a_mat, lvl)


def _first_index_of_max(x, idx):
    m = jnp.max(x, axis=0, keepdims=True)
    i = jnp.min(jnp.where(x == m, idx, EXPERTS_PER_GROUP), axis=0, keepdims=True)
    return m, i


def _merge_kernel(a_ref, o2_ref, g_ref, ga_ref, gb_ref, h0_ref, ng_ref,
                  wa_ref, wb_ref, wo_ref, ln1g_ref, ln1b_ref, wrh_ref, wrl_ref, br_ref,
                  h1_ref, ri_ref, rw_ref, xs_ref, pc_ref, *, heads, alpha):
    o = o2_ref[0].astype(F32) + o2_ref[1].astype(F32)
    gate = g_ref[...].astype(F32)
    parts = []
    for h in range(heads):
        cols = slice(h * LANES, (h + 1) * LANES)
        oh = o[:, cols]
        ms = jnp.mean(oh * oh, axis=-1, keepdims=True)
        gh = gate[:, cols]
        parts.append((oh * lax.rsqrt(ms + RMS_EPS) * ng_ref[:, cols] * (gh * _sigmoid(gh))).astype(BF16))
    c = jnp.concatenate(parts, axis=-1)
    pa = jnp.dot(a_ref[...], wa_ref[...], preferred_element_type=F32)
    pb = jnp.dot(c, wb_ref[...], preferred_element_type=F32)
    merged = _sigmoid(ga_ref[...].astype(F32)) * pa + _sigmoid(gb_ref[...].astype(F32)) * pb
    y = jnp.dot(merged.astype(BF16), wo_ref[...], preferred_element_type=F32)
    h1 = _ln(alpha * h0_ref[...] + y, ln1g_ref[...], ln1b_ref[...])
    h1_ref[...] = h1

    nt = (((1,), (1,)), ((), ()))
    h_hi = h1.astype(BF16)
    h_lo = (h1 - h_hi.astype(F32)).astype(BF16)
    logits = (lax.dot_general(wrh_ref[...], h_hi, nt, preferred_element_type=F32)
              + lax.dot_general(wrh_ref[...], h_lo, nt, preferred_element_type=F32)
              + lax.dot_general(wrl_ref[...], h_hi, nt, preferred_element_type=F32) + br_ref[:, 0:1])
    n = logits.shape[1]
    idx = lax.broadcasted_iota(jnp.int32, (EXPERTS_PER_GROUP, n), 0)
    glog = jnp.where(idx < N_GROUPS, logits[0:EXPERTS_PER_GROUP], -jnp.inf)
    gmax, gsel = _first_index_of_max(glog, idx)
    gw = 1.0 / jnp.sum(jnp.exp(glog - gmax), axis=0, keepdims=True)
    elog = logits[EXPERTS_PER_GROUP:2 * EXPERTS_PER_GROUP]
    for gi in range(1, N_GROUPS):
        lo = EXPERTS_PER_GROUP * (gi + 1)
        elog = jnp.where(gsel == gi, logits[lo:lo + EXPERTS_PER_GROUP], elog)
    ee = jnp.exp(elog - jnp.max(elog, axis=0, keepdims=True))
    p = ee / jnp.sum(ee, axis=0, keepdims=True)
    p1, i1 = _first_index_of_max(p, idx)
    p2, i2 = _first_index_of_max(jnp.where(idx == i1, -1.0, p), idx)
    denom = p1 + p2
    e1 = gsel * EXPERTS_PER_GROUP + i1
    e2 = gsel * EXPERTS_PER_GROUP + i2
    rw_ref[...] = jnp.where(idx == 0, gw * p1 / denom, jnp.where(idx == 1, gw * p2 / denom, 0.0))

    ex = lax.broadcasted_iota(jnp.int32, (N_EXPERTS, n), 0)
    m1 = ex == e1
    m2 = ex == e2
    member = jnp.where(m1, 1.0, jnp.where(m2, 1.0, 0.0))
    before = (lax.broadcasted_iota(jnp.int32, (n, n), 0) < lax.broadcasted_iota(jnp.int32, (n, n), 1))
    rank = jnp.dot(member.astype(BF16), jnp.where(before, 1.0, 0.0).astype(BF16), preferred_element_type=F32)
    count = jnp.sum(member, axis=1, keepdims=True)
    chunks = jnp.floor((count + (MOE_CHUNK - 1)) * (1.0 / MOE_CHUNK))
    chunks_b = jnp.broadcast_to(chunks, (N_EXPERTS, LANES))
    lower = (lax.broadcasted_iota(jnp.int32, (N_EXPERTS, N_EXPERTS), 1)
             < lax.broadcasted_iota(jnp.int32, (N_EXPERTS, N_EXPERTS), 0))
    seg = jnp.dot(jnp.where(lower, 1.0, 0.0), chunks_b, precision=lax.Precision.HIGHEST,
                  preferred_element_type=F32)
    posmat = seg[:, 0:1] * MOE_CHUNK + rank
    pos1 = jnp.sum(jnp.where(m1, posmat, 0.0), axis=0, keepdims=True).astype(jnp.int32)
    pos2 = jnp.sum(jnp.where(m2, posmat, 0.0), axis=0, keepdims=True).astype(jnp.int32)
    ri_ref[...] = jnp.where(idx == 0, pos1, jnp.where(idx == 1, pos2, 0))
    pc_ref[...] = chunks_b
    slot = lax.broadcasted_iota(jnp.int32, (xs_ref.shape[0], n), 0)
    perm = jnp.where(slot == pos1, 1.0, jnp.where(slot == pos2, 1.0, 0.0)).astype(BF16)
    xs_ref[...] = jnp.dot(perm, h_hi, preferred_element_type=F32)


def _merge(a, o2, rest, h0, norm_g, wa, wb, wo, ln1g, ln1b, wr_t, br_col, *, heads, alpha):
    t, d = h0.shape
    w = heads * LANES
    wa_w = a.shape[1]
    tm = MOE_TM
    n_tiles = t // tm
    wr_hi = wr_t.astype(BF16)
    wr_lo = (wr_t - wr_hi.astype(F32)).astype(BF16)
    full = lambda shape: pl.BlockSpec(shape, lambda i: (0,) * len(shape))
    return pl.pallas_call(
        functools.partial(_merge_kernel, heads=heads, alpha=alpha),
        grid=(t // tm,),
        in_specs=[
            pl.BlockSpec((tm, wa_w), lambda i: (i, 0)),
            pl.BlockSpec((2, tm, w), lambda i: (0, i, 0)),
            pl.BlockSpec((tm, w), lambda i: (i, 1)),
            pl.BlockSpec((tm, d), lambda i: (i, 1)),
            pl.BlockSpec((tm, d), lambda i: (i, 2)),
            pl.BlockSpec((tm, d), lambda i: (i, 0)),
            full((1, w)),
            full(wa.shape), full(wb.shape), full(wo.shape),
            full((1, d)), full((1, d)),
            full(wr_t.shape), full(wr_t.shape), full(br_col.shape),
        ],
        out_specs=[
            pl.BlockSpec((tm, d), lambda i: (i, 0)),
            pl.BlockSpec((EXPERTS_PER_GROUP, tm), lambda i: (0, i)),
            pl.BlockSpec((EXPERTS_PER_GROUP, tm), lambda i: (0, i)),
            pl.BlockSpec((MOE_CAP, d), lambda i: (i, 0)),
            pl.BlockSpec((N_EXPERTS, LANES), lambda i: (i, 0)),
        ],
        out_shape=[
            jax.ShapeDtypeStruct((t, d), F32),
            jax.ShapeDtypeStruct((EXPERTS_PER_GROUP, t), jnp.int32),
            jax.ShapeDtypeStruct((EXPERTS_PER_GROUP, t), F32),
            jax.ShapeDtypeStruct((n_tiles * MOE_CAP, d), F32),
            jax.ShapeDtypeStruct((n_tiles * N_EXPERTS, LANES), F32),
        ],
        compiler_params=pltpu.CompilerParams(
            dimension_semantics=("parallel",), vmem_limit_bytes=VMEM_LIMIT),
        name="merge_outproj_router",
    )(a, o2, rest, rest, rest, h0, norm_g.reshape(1, w),
      wa, wb, wo, ln1g.reshape(1, d), ln1b.reshape(1, d), wr_hi, wr_lo, br_col)


def _expert_kernel(be_ref, nv_ref, cur_ref, nxt_ref, xy_in, wg_ref, wu_ref, wd_ref, xy_out,
                   xbuf, ybuf, wg_s, wu_s, wd_s, gsem, ssem):
    i = pl.program_id(0)
    n = pl.num_programs(0)
    slot = i % 2
    nv = nv_ref[i]

    def gather_copy(tbl_ref, s, buf_slot):
        src = pl.multiple_of(tbl_ref[0, 0, s] * MOE_CHUNK, MOE_CHUNK)
        dst = pl.multiple_of(s * MOE_CHUNK, MOE_CHUNK)
        return pltpu.make_async_copy(xy_in.at[pl.ds(src, MOE_CHUNK), :],
                                     xbuf.at[buf_slot, pl.ds(dst, MOE_CHUNK), :], gsem.at[buf_slot])

    def scatter_copy(s, buf_slot):
        src = pl.multiple_of(s * MOE_CHUNK, MOE_CHUNK)
        dst = pl.multiple_of(cur_ref[0, 0, s] * MOE_CHUNK, MOE_CHUNK)
        return pltpu.make_async_copy(ybuf.at[buf_slot, pl.ds(src, MOE_CHUNK), :],
                                     xy_out.at[pl.ds(dst, MOE_CHUNK), :], ssem.at[buf_slot])

    def for_each_chunk(count, fn):
        for g0 in range(0, MOE_CPB, MOE_DMA_GROUP):
            @pl.when(count >= g0 + MOE_DMA_GROUP)
            def _():
                for s in range(g0, g0 + MOE_DMA_GROUP):
                    fn(s)

        def body(s, c):
            fn(s)
            return c
        lax.fori_loop(count // MOE_DMA_GROUP * MOE_DMA_GROUP, count, body, 0)

    def start_gathers(tbl_ref, buf_slot):
        for s in range(MOE_CPB):
            gather_copy(tbl_ref, s, buf_slot).start()

    def wait_scatters(count, buf_slot):
        for_each_chunk(count, lambda s: scatter_copy(0, buf_slot).wait())

    @pl.when((i == 0) & (nv > 0))
    def _():
        start_gathers(cur_ref, slot)

    nxt = jnp.minimum(i + 1, n - 1)

    @pl.when((i + 1 < n) & (nv_ref[nxt] > 0))
    def _():
        start_gathers(nxt_ref, 1 - slot)

    @pl.when(i >= 2)
    def _():
        wait_scatters(nv_ref[jnp.maximum(i - 2, 0)], slot)

    @pl.when(nv > 0)
    def _():
        for _ in range(MOE_CPB):
            gather_copy(cur_ref, 0, slot).wait()

        @pl.when((i == 0) | (be_ref[i] != be_ref[jnp.maximum(i - 1, 0)]))
        def _():
            wg_s[...] = wg_ref[0].astype(BF16)
            wu_s[...] = wu_ref[0].astype(BF16)
            wd_s[...] = wd_ref[0].astype(BF16)

        x = xbuf[slot].astype(BF16)
        gate = jnp.dot(x, wg_s[...], preferred_element_type=F32)
        up = jnp.dot(x, wu_s[...], preferred_element_type=F32)
        hid = (gate * _sigmoid(gate) * up).astype(BF16)
        ybuf[slot] = jnp.dot(hid, wd_s[...], preferred_element_type=F32)

        for_each_chunk(nv, lambda s: scatter_copy(s, slot).start())

    @pl.when(i == n - 1)
    def _():
        @pl.when(i >= 1)
        def _():
            wait_scatters(nv_ref[jnp.maximum(i - 1, 0)], 1 - slot)
        wait_scatters(nv, slot)


def _experts(xy, block_expert, block_nvalid, table, wg, wu, wd):
    rows, d = xy.shape
    n_blocks = table.shape[0]
    ne, _, dff = wg.shape
    return pl.pallas_call(
        _expert_kernel,
        grid_spec=pltpu.PrefetchScalarGridSpec(
            num_scalar_prefetch=2,
            grid=(n_blocks,),
            in_specs=[
                pl.BlockSpec((1, 1, MOE_CPB), lambda i, be, nv: (i, 0, 0), memory_space=pltpu.SMEM),
                pl.BlockSpec((1, 1, MOE_CPB), lambda i, be, nv: (jnp.minimum(i + 1, n_blocks - 1), 0, 0),
                             memory_space=pltpu.SMEM),
                pl.BlockSpec(memory_space=pl.ANY),
                pl.BlockSpec((1, d, dff), lambda i, be, nv: (be[i], 0, 0)),
                pl.BlockSpec((1, d, dff), lambda i, be, nv: (be[i], 0, 0)),
                pl.BlockSpec((1, dff, d), lambda i, be, nv: (be[i], 0, 0)),
            ],
            out_specs=pl.BlockSpec(memory_space=pl.ANY),
            scratch_shapes=[
                pltpu.VMEM((2, MOE_BM, d), F32),
                pltpu.VMEM((2, MOE_BM, d), F32),
                pltpu.VMEM((d, dff), BF16),
                pltpu.VMEM((d, dff), BF16),
                pltpu.VMEM((dff, d), BF16),
                pltpu.SemaphoreType.DMA((2,)),
                pltpu.SemaphoreType.DMA((2,)),
            ],
        ),
        out_shape=jax.ShapeDtypeStruct((rows, d), F32),
        input_output_aliases={4: 0},
        compiler_params=pltpu.CompilerParams(
            dimension_semantics=("arbitrary",), vmem_limit_bytes=VMEM_LIMIT),
        name="expert_ffn",
    )(block_expert, block_nvalid, table, table, xy, wg, wu, wd)


def _chunk_plan(tile_chunks):
    n_tiles = tile_chunks.shape[0]
    cap_chunks = MOE_CAP // MOE_CHUNK
    max_tile_chunks = (TOP_K * MOE_TM + N_EXPERTS * (MOE_CHUNK - 1)) // MOE_CHUNK
    n_blocks = -(-n_tiles * max_tile_chunks // MOE_CPB) + N_EXPERTS
    pc = tile_chunks.astype(jnp.int32)
    seg = jnp.cumsum(pc, axis=1) - pc
    tile_end = jnp.cumsum(pc.T, axis=1)
    tile_beg = tile_end - pc.T
    tot = tile_end[:, -1]
    nblk = (tot + MOE_CPB - 1) // MOE_CPB
    bend = jnp.cumsum(nblk)
    bbeg = bend - nblk
    bidx = jnp.arange(n_blocks, dtype=jnp.int32)
    be = jnp.minimum(jnp.sum((bend[None, :] <= bidx[:, None]).astype(jnp.int32), axis=1), N_EXPERTS - 1)
    onehot_e = (be[:, None] == jnp.arange(N_EXPERTS, dtype=jnp.int32)[None, :]).astype(jnp.int32)
    pick = lambda vec: jnp.sum(onehot_e * vec[None, :], axis=1)
    q = ((bidx - pick(bbeg)) * MOE_CPB)[:, None] + jnp.arange(MOE_CPB, dtype=jnp.int32)[None, :]
    valid = (q < pick(tot)[:, None]) & (bidx < bend[-1])[:, None]
    rows_of = lambda tbl: jnp.sum(onehot_e[:, :, None] * tbl[None, :, :], axis=1)
    blk_end = rows_of(tile_end)
    base = jnp.arange(n_tiles, dtype=jnp.int32)[None, :] * cap_chunks + seg.T - tile_beg
    blk_base = rows_of(base)
    tile = jnp.sum((blk_end[:, None, :] <= q[:, :, None]).astype(jnp.int32), axis=2)
    tile = jnp.minimum(tile, n_tiles - 1)
    onehot_t = tile[:, :, None] == jnp.arange(n_tiles, dtype=jnp.int32)[None, None, :]
    src = q + jnp.sum(jnp.where(onehot_t, blk_base[:, None, :], 0), axis=2)
    src = jnp.where(valid, src, src[:, 0:1])
    nvalid = jnp.sum(valid, axis=1).astype(jnp.int32)
    src = jnp.where(nvalid[:, None] > 0, src, 0).astype(jnp.int32)
    return be.astype(jnp.int32), nvalid, src.reshape(n_blocks, 1, MOE_CPB)


def _final_kernel(h1_ref, ys_ref, pos_ref, w_ref, g_ref, b_ref, o_ref, *, alpha):
    cap = ys_ref.shape[0]
    n = h1_ref.shape[0]
    slot = lax.broadcasted_iota(jnp.int32, (cap, n), 0)
    first = slot == pos_ref[0:1, :]
    second = slot == pos_ref[1:2, :]
    row_w = jnp.sum(jnp.where(first, w_ref[0:1, :], jnp.where(second, w_ref[1:2, :], 0.0)), axis=1, keepdims=True)
    unsort = jnp.where(first, 1.0, jnp.where(second, 1.0, 0.0)).astype(BF16)
    ys = (ys_ref[...] * row_w).astype(BF16)
    moe = lax.dot_general(unsort, ys, (((0,), (0,)), ((), ())), preferred_element_type=F32)
    o_ref[...] = _ln(alpha * h1_ref[...] + moe, g_ref[...], b_ref[...])


def _final(h1, ys, pos, wts, g, b, *, alpha):
    t, d = h1.shape
    tm = MOE_TM
    return pl.pallas_call(
        functools.partial(_final_kernel, alpha=alpha),
        grid=(t // tm,),
        in_specs=[
            pl.BlockSpec((tm, d), lambda i: (i, 0)),
            pl.BlockSpec((MOE_CAP, d), lambda i: (i, 0)),
            pl.BlockSpec((EXPERTS_PER_GROUP, tm), lambda i: (0, i)),
            pl.BlockSpec((EXPERTS_PER_GROUP, tm), lambda i: (0, i)),
            pl.BlockSpec((1, d), lambda i: (0, 0)),
            pl.BlockSpec((1, d), lambda i: (0, 0)),
        ],
        out_specs=pl.BlockSpec((tm, d), lambda i: (i, 0)),
        out_shape=jax.ShapeDtypeStruct((t, d), F32),
        compiler_params=pltpu.CompilerParams(
            dimension_semantics=("parallel",), vmem_limit_bytes=VMEM_LIMIT),
        name="combine_postnorm",
    )(h1, ys, pos, wts, g.reshape(1, d), b.reshape(1, d))


def kernel(x, emb_ln_g, emb_ln_b, w_in, na_rpb, hg_lb, hg_norm_g, w_proj_a, w_proj_b, w_out, ln1_g, ln1_b,
           w_router_group, b_router_group, w_router_expert, b_router_expert, w_gate, w_up, w_down, ln2_g, ln2_b):
    batch, seq, d = x.shape
    depth = w_in.shape[0]
    t = batch * seq
    na_w = na_rpb.shape[1] * NA_HEAD_DIM
    hg_w = hg_norm_g.shape[1]
    hg_heads = hg_w // HG_DK
    alpha = float((2 * depth) ** 0.25)
    lb_all = jnp.cumsum(jax.nn.softmax(hg_lb.astype(F32), axis=1), axis=1)

    assert depth == 1, depth
    l = 0
    x2d = x.reshape(t, d)
    h0, na, hg, rest = _inproj(x2d, emb_ln_g, emb_ln_b, w_in[l].astype(BF16), lb_all[:, l], n_na=3 * na_w, hg_w=hg_w,
                           q_cols=na_w, q_scale=float(NA_HEAD_DIM ** -0.5 * LOG2_E))
    bias = _na_bias_tables(na_rpb[l], seq // GRID_W)
    a = _na(na.reshape(batch, seq, 3 * na_w), bias, batch=batch, seq=seq).reshape(t, na_w)
    o2 = _hgrn(hg, rest, batch=batch, seq=seq, heads=hg_heads)

    wr_t = jnp.zeros((LANES, d), F32)
    wr_t = wr_t.at[0:N_GROUPS].set(w_router_group[l].T.astype(F32))
    wr_t = wr_t.at[EXPERTS_PER_GROUP:EXPERTS_PER_GROUP + N_EXPERTS].set(w_router_expert[l].T.astype(F32))
    br = jnp.zeros((LANES,), F32)
    br = br.at[0:N_GROUPS].set(b_router_group[l].astype(F32))
    br = br.at[EXPERTS_PER_GROUP:EXPERTS_PER_GROUP + N_EXPERTS].set(b_router_expert[l].astype(F32))
    br_col = jnp.broadcast_to(br[:, None], (LANES, LANES))

    h1, pos, route_w, xs, tile_chunks = _merge(
        a, o2, rest, h0, hg_norm_g[l], w_proj_a[l].astype(BF16), w_proj_b[l].astype(BF16),
        w_out[l].astype(BF16), ln1_g[l], ln1_b[l], wr_t, br_col, heads=hg_heads, alpha=alpha)

    plan = _chunk_plan(tile_chunks[:, 0].reshape(t // MOE_TM, N_EXPERTS))
    ys = _experts(xs, *plan, w_gate[l], w_up[l], w_down[l])
    h2 = _final(h1, ys, pos, route_w, ln2_g[l], ln2_b[l], alpha=alpha)
    return h2.reshape(batch, seq, d)
```

```python
import functools

import numpy as np
import jax
import jax.numpy as jnp
from jax import lax
from jax.experimental import pallas as pl
from jax.experimental.pallas import tpu as pltpu

F32 = jnp.float32
BF16 = jnp.bfloat16

GRID_W = 64
NA_HEAD_DIM = 64
NA_KH = 8
NA_KW = 16
HG_DK = 128
N_GROUPS = 4
EXPERTS_PER_GROUP = 8
N_EXPERTS = N_GROUPS * EXPERTS_PER_GROUP
TOP_K = 2
LN_EPS = 1e-5
RMS_EPS = 1e-6

LANES = 128
SUBLANES = 8
VMEM_LIMIT = 56 * 1024 * 1024
NEG_BIG = -1e30
LOG2_E = 1.4426950408889634

NA_QROWS = 4
NA_KROWS = NA_QROWS + NA_KH
HG_CHUNK = 128
HG_LEVELS = 7
HG_PAIR = 2
MOE_TM = 512
MOE_CHUNK = 8
MOE_CAP = TOP_K * MOE_TM + N_EXPERTS * MOE_CHUNK
MOE_BM = 256
MOE_CPB = MOE_BM // MOE_CHUNK
MOE_DMA_GROUP = 8


def _ln(x, g, b):
    mu = jnp.mean(x, axis=-1, keepdims=True)
    xc = x - mu
    var = jnp.mean(xc * xc, axis=-1, keepdims=True)
    return xc * lax.rsqrt(var + LN_EPS) * g + b


def _sigmoid(x):
    return 0.5 * jnp.tanh(0.5 * x) + 0.5


def _inproj_kernel(x_ref, g_ref, b_ref, w_ref, lb_ref, h0_ref, na_ref, hg_ref, rest_ref, *,
                   n_na, hg_w, q_cols, q_scale):
    h0 = _ln(x_ref[...], g_ref[...], b_ref[...])
    h0_ref[...] = h0
    h = h0.astype(BF16)
    d_in = w_ref.shape[1]
    cn = hg_w
    for c0 in range(0, d_in, cn):
        u = jnp.dot(h, w_ref[:, c0:c0 + cn], preferred_element_type=F32)
        if c0 < q_cols:
            u = u * q_scale
        if c0 < n_na:
            na_ref[:, c0:c0 + cn] = u.astype(na_ref.dtype)
        elif c0 == n_na:
            hg_ref[:, 0:cn] = (u * _sigmoid(u)).astype(hg_ref.dtype)
        elif c0 < n_na + 3 * hg_w:
            d = (c0 - n_na) // hg_w - 1
            lb = lb_ref[d]
            forget = lb + (1.0 - lb) * _sigmoid(u)
            logf = jnp.log(forget)
            hi = logf.astype(BF16)
            hg_ref[:, (1 + d) * cn:(2 + d) * cn] = (1.0 - forget).astype(hg_ref.dtype)
            hg_ref[:, (3 + 2 * d) * cn:(4 + 2 * d) * cn] = hi
            hg_ref[:, (4 + 2 * d) * cn:(5 + 2 * d) * cn] = (logf - hi.astype(F32)).astype(BF16)
        else:
            o = c0 - n_na - 3 * hg_w
            rest_ref[:, o:o + cn] = u.astype(rest_ref.dtype)


def _inproj(x2d, g, b, w_bf16, lb, *, n_na, hg_w, q_cols, q_scale, tm=512):
    t, d = x2d.shape
    d_in = w_bf16.shape[1]
    n_rest = d_in - n_na - 3 * hg_w
    kern = functools.partial(_inproj_kernel, n_na=n_na, hg_w=hg_w, q_cols=q_cols, q_scale=q_scale)
    return pl.pallas_call(
        kern,
        grid=(t // tm,),
        in_specs=[
            pl.BlockSpec((tm, d), lambda i: (i, 0)),
            pl.BlockSpec((1, d), lambda i: (0, 0)),
            pl.BlockSpec((1, d), lambda i: (0, 0)),
            pl.BlockSpec((d, d_in), lambda i: (0, 0), pipeline_mode=pl.Buffered(1)),
            pl.BlockSpec((2, 1, hg_w), lambda i: (0, 0, 0)),
        ],
        out_specs=[
            pl.BlockSpec((tm, d), lambda i: (i, 0)),
            pl.BlockSpec((tm, n_na), lambda i: (i, 0)),
            pl.BlockSpec((tm, 7 * hg_w), lambda i: (i, 0)),
            pl.BlockSpec((tm, n_rest), lambda i: (i, 0)),
        ],
        out_shape=[
            jax.ShapeDtypeStruct((t, d), F32),
            jax.ShapeDtypeStruct((t, n_na), BF16),
            jax.ShapeDtypeStruct((t, 7 * hg_w), BF16),
            jax.ShapeDtypeStruct((t, n_rest), BF16),
        ],
        compiler_params=pltpu.CompilerParams(
            dimension_semantics=("parallel",), vmem_limit_bytes=VMEM_LIMIT),
        name="ln_inproj",
    )(x2d, g.reshape(1, d), b.reshape(1, d), w_bf16, lb.reshape(2, 1, hg_w))


def _na_bias_tables(rpb, rows):
    nq = NA_QROWS * GRID_W
    nk = NA_KROWS * GRID_W
    qi = np.arange(nq)
    kj = np.arange(nk)
    qc = (qi % GRID_W)[:, None]
    kc = (kj % GRID_W)[None, :]
    cs = np.clip(qc - NA_KW // 2, 0, GRID_W - NA_KW)
    valid_c = (kc >= cs) & (kc < cs + NA_KW)
    n_rel_r, n_rel_c = 2 * NA_KH - 1, 2 * NA_KW - 1
    wc = np.arange(GRID_W)
    cidx = np.clip(wc[None, :] - wc[:, None] + NA_KW - 1, 0, n_rel_c - 1)
    onehot_c = (cidx[None] == np.arange(n_rel_c)[:, None, None]).astype(np.float32)
    ridx_all, valid_all = [], []
    for r0 in (0, NA_QROWS, rows - NA_QROWS):
        start = int(np.clip(r0 - NA_KH // 2, 0, rows - NA_KROWS))
        qr = (r0 + qi // GRID_W)[:, None]
        kr = (start + kj // GRID_W)[None, :]
        rs = np.clip(qr - NA_KH // 2, 0, rows - NA_KH)
        valid_all.append((kr >= rs) & (kr < rs + NA_KH) & valid_c)
        qr_s = r0 + np.arange(NA_QROWS)[:, None]
        kr_s = start + np.arange(NA_KROWS)[None, :]
        ridx_all.append(np.clip(kr_s - qr_s + NA_KH - 1, 0, n_rel_r - 1))
    valid = np.stack(valid_all)
    rows_sel = rpb.astype(F32)[:, np.stack(ridx_all), :]
    bias = jnp.einsum('hcqkj,jxy->chqxky', rows_sel, onehot_c, precision=lax.Precision.HIGHEST)
    bias = bias.reshape(3, rpb.shape[0], nq, nk) * LOG2_E
    return jnp.where(valid[:, None], bias, NEG_BIG)


def _na_kernel(q_ref, k_ref, v_ref, bias_ref, o_ref, kt_ref, *, rows):
    kt_ref[...] = k_ref[0].T
    nq = NA_QROWS * GRID_W
    nk = NA_KROWS * GRID_W
    nrb = rows // NA_QROWS
    half = nq // 2
    lane = lax.broadcasted_iota(jnp.int32, (1, LANES), 1)
    first = lane < NA_HEAD_DIM
    work = [(hh, r0) for hh in range(2) for r0 in (0, half)]

    def row_block(rb, carry):
        start = jnp.clip(rb * NA_QROWS - NA_KH // 2, 0, rows - NA_KROWS)
        kstart = pl.multiple_of(start * GRID_W, NA_QROWS * GRID_W)
        qstart = pl.multiple_of(rb * nq, nq)
        cls = jnp.where(rb == 0, 0, jnp.where(rb == nrb - 1, 2, 1))
        kwin_t = kt_ref[:, pl.ds(kstart, nk)]
        vwin = v_ref[0, pl.ds(kstart, nk), :]
        q = q_ref[0, pl.ds(qstart, nq), :]
        zero = jnp.zeros_like(q)
        qs = [jnp.where(first, q, zero), jnp.where(first, zero, q)]
        ss = [jnp.dot(qs[hh][r0:r0 + half], kwin_t, preferred_element_type=F32)
              + bias_ref[cls, 0, hh, r0:r0 + half, :] for hh, r0 in work]
        ps = [jnp.exp2(s - jnp.max(s, axis=-1, keepdims=True)) for s in ss]
        ls = [jnp.sum(p, axis=-1, keepdims=True) for p in ps]
        os = [jnp.dot(p.astype(BF16), vwin, preferred_element_type=F32) / l for p, l in zip(ps, ls)]
        o_ref[0, pl.ds(qstart, nq), :] = jnp.where(first, jnp.concatenate(os[0:2], axis=0),
                                                   jnp.concatenate(os[2:4], axis=0)).astype(o_ref.dtype)
        return carry

    lax.fori_loop(0, nrb, row_block, 0)


def _na(qkv, bias, *, batch, seq):
    rows = seq // GRID_W
    width = qkv.shape[-1] // 3
    pairs = width // LANES
    nq = NA_QROWS * GRID_W
    nk = NA_KROWS * GRID_W
    bias5 = bias.reshape(3, pairs, 2, nq, nk)
    return pl.pallas_call(
        functools.partial(_na_kernel, rows=rows),
        grid=(pairs, batch),
        in_specs=[
            pl.BlockSpec((1, seq, LANES), lambda p, b: (b, 0, p)),
            pl.BlockSpec((1, seq, LANES), lambda p, b: (b, 0, pairs + p)),
            pl.BlockSpec((1, seq, LANES), lambda p, b: (b, 0, 2 * pairs + p)),
            pl.BlockSpec((3, 1, 2, nq, nk), lambda p, b: (0, p, 0, 0, 0)),
        ],
        out_specs=pl.BlockSpec((1, seq, LANES), lambda p, b: (b, 0, p)),
        out_shape=jax.ShapeDtypeStruct((batch, seq, width), BF16),
        scratch_shapes=[pltpu.VMEM((LANES, seq), BF16)],
        compiler_params=pltpu.CompilerParams(
            dimension_semantics=("parallel", "parallel"), vmem_limit_bytes=VMEM_LIMIT),
        name="neighbourhood_attention",
    )(qkv, qkv, qkv, bias5)


def _hg_tables():
    c, nl = HG_CHUNK, HG_LEVELS
    a = np.zeros((2, c * (nl + 1) + 8, c), np.float32)
    lvl = np.full((2, c, c), -1, np.int32)
    for rev in (0, 1):
        for t in range(c):
            if rev == 0:
                a[rev, t, :t + 1] = 1.0
            else:
                a[rev, t, t:] = 1.0
            lvl[rev, t, t] = nl
            for s in range(c):
                if (s < t and rev == 0) or (s > t and rev == 1):
                    lvl[rev, t, s] = int(np.floor(np.log2(t ^ s)))
            for l in range(nl):
                m = 1 << l
                mid = (t & ~(2 * m - 1)) + m
                upper = (t >> l) & 1
                row = c * (l + 1) + t
                if rev == 0:
                    lo, hi = (mid, t) if upper else (t + 1, mid - 1)
                else:
                    lo, hi = (mid, t - 1) if upper else (t, mid - 1)
                if hi >= lo:
                    a[rev, row, lo:hi + 1] = 1.0
        a[rev, c * (nl + 1):, :] = 1.0
    return a, lvl


def _hg_kernel(q_ref, k_ref, up_ref, dn_ref, ghi_ref, glo_ref, i_ref, a_ref, lvl_ref, o_ref, st_ref, xt_ref, *,
               heads, n_chunks):
    rev = pl.program_id(1)
    c, nl = HG_CHUNK, HG_LEVELS

    @pl.when(pl.program_id(2) == 0)
    def _():
        st_ref[...] = jnp.zeros_like(st_ref)

    a_mat = a_ref[0]
    lvl = lvl_ref[0]
    w = heads * LANES
    sub = lax.broadcasted_iota(jnp.int32, (SUBLANES, w), 0)
    nt = (((1,), (1,)), ((), ()))
    tn = (((0,), (0,)), ((), ()))
    head_cols = [slice(h * LANES, (h + 1) * LANES) for h in range(heads)]

    def level_operand(l, up, dn):
        m = 1 << l
        if m >= SUBLANES:
            return jnp.concatenate([(up if k & 1 else dn)[k * m:(k + 1) * m] for k in range(c // m)], axis=0)
        upper = ((sub >> l) & 1) == 1
        shape3 = (c // SUBLANES, SUBLANES, w)
        return jnp.where(upper[None], up.reshape(shape3), dn.reshape(shape3)).reshape(c, w)

    def chunk(j, u):
        cj = jnp.where(rev == 1, n_chunks - 1 - j, j)
        r0 = pl.multiple_of(cj * c, c)
        rows = pl.ds(r0, c)
        qb = q_ref[rows, :]
        kb = k_ref[rows, :]
        q = qb.astype(F32)
        kk = kb.astype(F32)
        up = up_ref[rows, :].astype(F32)
        dn = dn_ref[rows, :].astype(F32)
        v = i_ref[rows, :]
        e = jnp.dot(a_mat, jnp.concatenate([ghi_ref[rows, :], glo_ref[rows, :]], axis=0),
                    preferred_element_type=F32)
        b = e[0:c]
        btot = e[c * (nl + 1):c * (nl + 1) + 1]
        qd = (q * jnp.exp(b)).astype(BF16)
        khat = (kk * jnp.exp(btot - b)).astype(BF16)
        decay = jnp.exp(btot)
        xs = [(level_operand(l, up, dn) * jnp.exp(e[c * (l + 1):c * (l + 2)])).astype(BF16) for l in range(nl)]
        xt_ref[u, nl] = kb.T
        attns = [jnp.where(lvl == nl, jnp.dot(qb[:, hc], xt_ref[u, nl, hc, :], preferred_element_type=F32), 0.0)
                 for hc in head_cols]
        for l in range(nl):
            x = xs[l]
            xt_ref[u, l] = x.T
            attns = [jnp.where(lvl == l, jnp.dot(x[:, hc], xt_ref[u, l, hc, :], preferred_element_type=F32), at)
                     for hc, at in zip(head_cols, attns)]
        intra = [jnp.dot(at.astype(BF16), v[:, hc], preferred_element_type=F32)
                 for hc, at in zip(head_cols, attns)]
        update = [lax.dot_general(v[:, hc], khat[:, hc], tn, preferred_element_type=F32) for hc in head_cols]
        return rows, qd, decay, intra, update

    def chunk_pair(jp, carry):
        parts = [chunk(jp * HG_PAIR + u, u) for u in range(HG_PAIR)]
        sts = [st_ref[h] for h in range(heads)]
        for rows, qd, decay, intra, update in parts:
            for h, hc in enumerate(head_cols):
                o_ref[0, rows, hc] = (intra[h] + lax.dot_general(qd[:, hc], sts[h].astype(BF16), nt,
                                                                 preferred_element_type=F32)).astype(o_ref.dtype)
            sts = [st * decay[:, hc] + up_h for st, hc, up_h in zip(sts, head_cols, update)]
        for h in range(heads):
            st_ref[h] = sts[h]
        return carry

    lax.fori_loop(0, n_chunks // HG_PAIR, chunk_pair, 0)


def _hgrn(hg, rest, *, batch, seq, heads, cb=1024):
    t = batch * seq
    w = heads * LANES
    nsb = seq // cb
    a_np, lvl_np = _hg_tables()
    a_np = np.concatenate([a_np, a_np], axis=-1)
    a_mat = jnp.asarray(a_np, BF16)
    lvl = jnp.asarray(lvl_np)

    def rowblk(b, d, s):
        return b * nsb + jnp.where(d == 1, nsb - 1 - s, s)

    def col(fn):
        return pl.BlockSpec((cb, w), lambda b, d, s: (rowblk(b, d, s), fn(d)))

    return pl.pallas_call(
        functools.partial(_hg_kernel, heads=heads, n_chunks=cb // HG_CHUNK),
        grid=(batch, 2, nsb),
        in_specs=[
            col(lambda d: 0),
            col(lambda d: 1 + d),
            col(lambda d: 2 * d),
            col(lambda d: 1 - d),
            col(lambda d: 3 + 2 * d),
            col(lambda d: 4 + 2 * d),
            col(lambda d: 0),
            pl.BlockSpec((1,) + a_np.shape[1:], lambda b, d, s: (d, 0, 0)),
            pl.BlockSpec((1,) + lvl_np.shape[1:], lambda b, d, s: (d, 0, 0)),
        ],
        out_specs=pl.BlockSpec((1, cb, w), lambda b, d, s: (d, rowblk(b, d, s), 0)),
        out_shape=jax.ShapeDtypeStruct((2, t, w), BF16),
        scratch_shapes=[pltpu.VMEM((heads, LANES, LANES), F32),
                        pltpu.VMEM((HG_PAIR, HG_LEVELS + 1, w, HG_CHUNK), BF16)],
        compiler_params=pltpu.CompilerParams(
            dimension_semantics=("parallel", "parallel", "arbitrary"), vmem_limit_bytes=VMEM_LIMIT),
        name="hgrn2_scan",
    )(hg, hg, hg, hg, hg, hg, rest, a_mat, lvl)


def _first_index_of_max(x, idx):
    m = jnp.max(x, axis=0, keepdims=True)
    i = jnp.min(jnp.where(x == m, idx, EXPERTS_PER_GROUP), axis=0, keepdims=True)
    return m, i


def _merge_kernel(a_ref, o2_ref, g_ref, ga_ref, gb_ref, h0_ref, ng_ref,
                  wa_ref, wb_ref, wo_ref, ln1g_ref, ln1b_ref, wrh_ref, wrl_ref, br_ref,
                  h1_ref, ri_ref, rw_ref, xs_ref, pc_ref, *, heads, alpha):
    o = o2_ref[0].astype(F32) + o2_ref[1].astype(F32)
    gate = g_ref[...].astype(F32)
    parts = []
    for h in range(heads):
        cols = slice(h * LANES, (h + 1) * LANES)
        oh = o[:, cols]
        ms = jnp.mean(oh * oh, axis=-1, keepdims=True)
        gh = gate[:, cols]
        parts.append((oh * lax.rsqrt(ms + RMS_EPS) * ng_ref[:, cols] * (gh * _sigmoid(gh))).astype(BF16))
    c = jnp.concatenate(parts, axis=-1)
    pa = jnp.dot(a_ref[...], wa_ref[...], preferred_element_type=F32)
    pb = jnp.dot(c, wb_ref[...], preferred_element_type=F32)
    merged = _sigmoid(ga_ref[...].astype(F32)) * pa + _sigmoid(gb_ref[...].astype(F32)) * pb
    y = jnp.dot(merged.astype(BF16), wo_ref[...], preferred_element_type=F32)
    h1 = _ln(alpha * h0_ref[...] + y, ln1g_ref[...], ln1b_ref[...])
    h1_ref[...] = h1

    nt = (((1,), (1,)), ((), ()))
    h_hi = h1.astype(BF16)
    h_lo = (h1 - h_hi.astype(F32)).astype(BF16)
    logits = (lax.dot_general(wrh_ref[...], h_hi, nt, preferred_element_type=F32)
              + lax.dot_general(wrh_ref[...], h_lo, nt, preferred_element_type=F32)
              + lax.dot_general(wrl_ref[...], h_hi, nt, preferred_element_type=F32) + br_ref[:, 0:1])
    n = logits.shape[1]
    idx = lax.broadcasted_iota(jnp.int32, (EXPERTS_PER_GROUP, n), 0)
    glog = jnp.where(idx < N_GROUPS, logits[0:EXPERTS_PER_GROUP], -jnp.inf)
    gmax, gsel = _first_index_of_max(glog, idx)
    gw = 1.0 / jnp.sum(jnp.exp(glog - gmax), axis=0, keepdims=True)
    elog = logits[EXPERTS_PER_GROUP:2 * EXPERTS_PER_GROUP]
    for gi in range(1, N_GROUPS):
        lo = EXPERTS_PER_GROUP * (gi + 1)
        elog = jnp.where(gsel == gi, logits[lo:lo + EXPERTS_PER_GROUP], elog)
    ee = jnp.exp(elog - jnp.max(elog, axis=0, keepdims=True))
    p = ee / jnp.sum(ee, axis=0, keepdims=True)
    p1, i1 = _first_index_of_max(p, idx)
    p2, i2 = _first_index_of_max(jnp.where(idx == i1, -1.0, p), idx)
    denom = p1 + p2
    e1 = gsel * EXPERTS_PER_GROUP + i1
    e2 = gsel * EXPERTS_PER_GROUP + i2
    rw_ref[...] = jnp.where(idx == 0, gw * p1 / denom, jnp.where(idx == 1, gw * p2 / denom, 0.0))

    ex = lax.broadcasted_iota(jnp.int32, (N_EXPERTS, n), 0)
    m1 = ex == e1
    m2 = ex == e2
    member = jnp.where(m1, 1.0, jnp.where(m2, 1.0, 0.0))
    before = (lax.broadcasted_iota(jnp.int32, (n, n), 0) < lax.broadcasted_iota(jnp.int32, (n, n), 1))
    rank = jnp.dot(member.astype(BF16), jnp.where(before, 1.0, 0.0).astype(BF16), preferred_element_type=F32)
    count = jnp.sum(member, axis=1, keepdims=True)
    chunks = jnp.floor((count + (MOE_CHUNK - 1)) * (1.0 / MOE_CHUNK))
    chunks_b = jnp.broadcast_to(chunks, (N_EXPERTS, LANES))
    lower = (lax.broadcasted_iota(jnp.int32, (N_EXPERTS, N_EXPERTS), 1)
             < lax.broadcasted_iota(jnp.int32, (N_EXPERTS, N_EXPERTS), 0))
    seg = jnp.dot(jnp.where(lower, 1.0, 0.0), chunks_b, precision=lax.Precision.HIGHEST,
                  preferred_element_type=F32)
    posmat = seg[:, 0:1] * MOE_CHUNK + rank
    pos1 = jnp.sum(jnp.where(m1, posmat, 0.0), axis=0, keepdims=True).astype(jnp.int32)
    pos2 = jnp.sum(jnp.where(m2, posmat, 0.0), axis=0, keepdims=True).astype(jnp.int32)
    ri_ref[...] = jnp.where(idx == 0, pos1, jnp.where(idx == 1, pos2, 0))
    pc_ref[...] = chunks_b
    slot = lax.broadcasted_iota(jnp.int32, (xs_ref.shape[0], n), 0)
    perm = jnp.where(slot == pos1, 1.0, jnp.where(slot == pos2, 1.0, 0.0)).astype(BF16)
    xs_ref[...] = jnp.dot(perm, h_hi, preferred_element_type=F32)


def _merge(a, o2, rest, h0, norm_g, wa, wb, wo, ln1g, ln1b, wr_t, br_col, *, heads, alpha):
    t, d = h0.shape
    w = heads * LANES
    wa_w = a.shape[1]
    tm = MOE_TM
    n_tiles = t // tm
    wr_hi = wr_t.astype(BF16)
    wr_lo = (wr_t - wr_hi.astype(F32)).astype(BF16)
    full = lambda shape: pl.BlockSpec(shape, lambda i: (0,) * len(shape))
    return pl.pallas_call(
        functools.partial(_merge_kernel, heads=heads, alpha=alpha),
        grid=(t // tm,),
        in_specs=[
            pl.BlockSpec((tm, wa_w), lambda i: (i, 0)),
            pl.BlockSpec((2, tm, w), lambda i: (0, i, 0)),
            pl.BlockSpec((tm, w), lambda i: (i, 1)),
            pl.BlockSpec((tm, d), lambda i: (i, 1)),
            pl.BlockSpec((tm, d), lambda i: (i, 2)),
            pl.BlockSpec((tm, d), lambda i: (i, 0)),
            full((1, w)),
            full(wa.shape), full(wb.shape), full(wo.shape),
            full((1, d)), full((1, d)),
            full(wr_t.shape), full(wr_t.shape), full(br_col.shape),
        ],
        out_specs=[
            pl.BlockSpec((tm, d), lambda i: (i, 0)),
            pl.BlockSpec((EXPERTS_PER_GROUP, tm), lambda i: (0, i)),
            pl.BlockSpec((EXPERTS_PER_GROUP, tm), lambda i: (0, i)),
            pl.BlockSpec((MOE_CAP, d), lambda i: (i, 0)),
            pl.BlockSpec((N_EXPERTS, LANES), lambda i: (i, 0)),
        ],
        out_shape=[
            jax.ShapeDtypeStruct((t, d), F32),
            jax.ShapeDtypeStruct((EXPERTS_PER_GROUP, t), jnp.int32),
            jax.ShapeDtypeStruct((EXPERTS_PER_GROUP, t), F32),
            jax.ShapeDtypeStruct((n_tiles * MOE_CAP, d), F32),
            jax.ShapeDtypeStruct((n_tiles * N_EXPERTS, LANES), F32),
        ],
        compiler_params=pltpu.CompilerParams(
            dimension_semantics=("parallel",), vmem_limit_bytes=VMEM_LIMIT),
        name="merge_outproj_router",
    )(a, o2, rest, rest, rest, h0, norm_g.reshape(1, w),
      wa, wb, wo, ln1g.reshape(1, d), ln1b.reshape(1, d), wr_hi, wr_lo, br_col)


def _expert_kernel(be_ref, nv_ref, cur_ref, nxt_ref, xy_in, wg_ref, wu_ref, wd_ref, xy_out,
                   xbuf, ybuf, wg_s, wu_s, wd_s, gsem, ssem):
    i = pl.program_id(0)
    n = pl.num_programs(0)
    slot = i % 2
    nv = nv_ref[i]

    def gather_copy(tbl_ref, s, buf_slot):
        src = pl.multiple_of(tbl_ref[0, 0, s] * MOE_CHUNK, MOE_CHUNK)
        dst = pl.multiple_of(s * MOE_CHUNK, MOE_CHUNK)
        return pltpu.make_async_copy(xy_in.at[pl.ds(src, MOE_CHUNK), :],
                                     xbuf.at[buf_slot, pl.ds(dst, MOE_CHUNK), :], gsem.at[buf_slot])

    def scatter_copy(s, buf_slot):
        src = pl.multiple_of(s * MOE_CHUNK, MOE_CHUNK)
        dst = pl.multiple_of(cur_ref[0, 0, s] * MOE_CHUNK, MOE_CHUNK)
        return pltpu.make_async_copy(ybuf.at[buf_slot, pl.ds(src, MOE_CHUNK), :],
                                     xy_out.at[pl.ds(dst, MOE_CHUNK), :], ssem.at[buf_slot])

    def for_each_chunk(count, fn):
        for g0 in range(0, MOE_CPB, MOE_DMA_GROUP):
            @pl.when(count >= g0 + MOE_DMA_GROUP)
            def _():
                for s in range(g0, g0 + MOE_DMA_GROUP):
                    fn(s)

        def body(s, c):
            fn(s)
            return c
        lax.fori_loop(count // MOE_DMA_GROUP * MOE_DMA_GROUP, count, body, 0)

    def start_gathers(tbl_ref, buf_slot):
        for s in range(MOE_CPB):
            gather_copy(tbl_ref, s, buf_slot).start()

    def wait_scatters(count, buf_slot):
        for_each_chunk(count, lambda s: scatter_copy(0, buf_slot).wait())

    @pl.when((i == 0) & (nv > 0))
    def _():
        start_gathers(cur_ref, slot)

    nxt = jnp.minimum(i + 1, n - 1)

    @pl.when((i + 1 < n) & (nv_ref[nxt] > 0))
    def _():
        start_gathers(nxt_ref, 1 - slot)

    @pl.when(i >= 2)
    def _():
        wait_scatters(nv_ref[jnp.maximum(i - 2, 0)], slot)

    @pl.when(nv > 0)
    def _():
        for _ in range(MOE_CPB):
            gather_copy(cur_ref, 0, slot).wait()

        @pl.when((i == 0) | (be_ref[i] != be_ref[jnp.maximum(i - 1, 0)]))
        def _():
            wg_s[...] = wg_ref[0].astype(BF16)
            wu_s[...] = wu_ref[0].astype(BF16)
            wd_s[...] = wd_ref[0].astype(BF16)

        x = xbuf[slot].astype(BF16)
        gate = jnp.dot(x, wg_s[...], preferred_element_type=F32)
        up = jnp.dot(x, wu_s[...], preferred_element_type=F32)
        hid = (gate * _sigmoid(gate) * up).astype(BF16)
        ybuf[slot] = jnp.dot(hid, wd_s[...], preferred_element_type=F32)

        for_each_chunk(nv, lambda s: scatter_copy(s, slot).start())

    @pl.when(i == n - 1)
    def _():
        @pl.when(i >= 1)
        def _():
            wait_scatters(nv_ref[jnp.maximum(i - 1, 0)], 1 - slot)
        wait_scatters(nv, slot)


def _experts(xy, block_expert, block_nvalid, table, wg, wu, wd):
    rows, d = xy.shape
    n_blocks = table.shape[0]
    ne, _, dff = wg.shape
    return pl.pallas_call(
        _expert_kernel,
        grid_spec=pltpu.PrefetchScalarGridSpec(
            num_scalar_prefetch=2,
            grid=(n_blocks,),
            in_specs=[
                pl.BlockSpec((1, 1, MOE_CPB), lambda i, be, nv: (i, 0, 0), memory_space=pltpu.SMEM),
                pl.BlockSpec((1, 1, MOE_CPB), lambda i, be, nv: (jnp.minimum(i + 1, n_blocks - 1), 0, 0),
                             memory_space=pltpu.SMEM),
                pl.BlockSpec(memory_space=pl.ANY),
                pl.BlockSpec((1, d, dff), lambda i, be, nv: (be[i], 0, 0)),
                pl.BlockSpec((1, d, dff), lambda i, be, nv: (be[i], 0, 0)),
                pl.BlockSpec((1, dff, d), lambda i, be, nv: (be[i], 0, 0)),
            ],
            out_specs=pl.BlockSpec(memory_space=pl.ANY),
            scratch_shapes=[
                pltpu.VMEM((2, MOE_BM, d), F32),
                pltpu.VMEM((2, MOE_BM, d), F32),
                pltpu.VMEM((d, dff), BF16),
                pltpu.VMEM((d, dff), BF16),
                pltpu.VMEM((dff, d), BF16),
                pltpu.SemaphoreType.DMA((2,)),
                pltpu.SemaphoreType.DMA((2,)),
            ],
        ),
        out_shape=jax.ShapeDtypeStruct((rows, d), F32),
        input_output_aliases={4: 0},
        compiler_params=pltpu.CompilerParams(
            dimension_semantics=("arbitrary",), vmem_limit_bytes=VMEM_LIMIT),
        name="expert_ffn",
    )(block_expert, block_nvalid, table, table, xy, wg, wu, wd)


def _chunk_plan(tile_chunks):
    n_tiles = tile_chunks.shape[0]
    cap_chunks = MOE_CAP // MOE_CHUNK
    max_tile_chunks = (TOP_K * MOE_TM + N_EXPERTS * (MOE_CHUNK - 1)) // MOE_CHUNK
    n_blocks = -(-n_tiles * max_tile_chunks // MOE_CPB) + N_EXPERTS
    pc = tile_chunks.astype(jnp.int32)
    seg = jnp.cumsum(pc, axis=1) - pc
    tile_end = jnp.cumsum(pc.T, axis=1)
    tile_beg = tile_end - pc.T
    tot = tile_end[:, -1]
    nblk = (tot + MOE_CPB - 1) // MOE_CPB
    bend = jnp.cumsum(nblk)
    bbeg = bend - nblk
    bidx = jnp.arange(n_blocks, dtype=jnp.int32)
    be = jnp.minimum(jnp.sum((bend[None, :] <= bidx[:, None]).astype(jnp.int32), axis=1), N_EXPERTS - 1)
    onehot_e = (be[:, None] == jnp.arange(N_EXPERTS, dtype=jnp.int32)[None, :]).astype(jnp.int32)
    pick = lambda vec: jnp.sum(onehot_e * vec[None, :], axis=1)
    q = ((bidx - pick(bbeg)) * MOE_CPB)[:, None] + jnp.arange(MOE_CPB, dtype=jnp.int32)[None, :]
    valid = (q < pick(tot)[:, None]) & (bidx < bend[-1])[:, None]
    rows_of = lambda tbl: jnp.sum(onehot_e[:, :, None] * tbl[None, :, :], axis=1)
    blk_end = rows_of(tile_end)
    base = jnp.arange(n_tiles, dtype=jnp.int32)[None, :] * cap_chunks + seg.T - tile_beg
    blk_base = rows_of(base)
    tile = jnp.sum((blk_end[:, None, :] <= q[:, :, None]).astype(jnp.int32), axis=2)
    tile = jnp.minimum(tile, n_tiles - 1)
    onehot_t = tile[:, :, None] == jnp.arange(n_tiles, dtype=jnp.int32)[None, None, :]
    src = q + jnp.sum(jnp.where(onehot_t, blk_base[:, None, :], 0), axis=2)
    src = jnp.where(valid, src, src[:, 0:1])
    nvalid = jnp.sum(valid, axis=1).astype(jnp.int32)
    src = jnp.where(nvalid[:, None] > 0, src, 0).astype(jnp.int32)
    return be.astype(jnp.int32), nvalid, src.reshape(n_blocks, 1, MOE_CPB)


def _final_kernel(h1_ref, ys_ref, pos_ref, w_ref, g_ref, b_ref, o_ref, *, alpha):
    cap = ys_ref.shape[0]
    n = h1_ref.shape[0]
    slot = lax.broadcasted_iota(jnp.int32, (cap, n), 0)
    first = slot == pos_ref[0:1, :]
    second = slot == pos_ref[1:2, :]
    row_w = jnp.sum(jnp.where(first, w_ref[0:1, :], jnp.where(second, w_ref[1:2, :], 0.0)), axis=1, keepdims=True)
    unsort = jnp.where(first, 1.0, jnp.where(second, 1.0, 0.0)).astype(BF16)
    ys = (ys_ref[...] * row_w).astype(BF16)
    moe = lax.dot_general(unsort, ys, (((0,), (0,)), ((), ())), preferred_element_type=F32)
    o_ref[...] = _ln(alpha * h1_ref[...] + moe, g_ref[...], b_ref[...])


def _final(h1, ys, pos, wts, g, b, *, alpha):
    t, d = h1.shape
    tm = MOE_TM
    return pl.pallas_call(
        functools.partial(_final_kernel, alpha=alpha),
        grid=(t // tm,),
        in_specs=[
            pl.BlockSpec((tm, d), lambda i: (i, 0)),
            pl.BlockSpec((MOE_CAP, d), lambda i: (i, 0)),
            pl.BlockSpec((EXPERTS_PER_GROUP, tm), lambda i: (0, i)),
            pl.BlockSpec((EXPERTS_PER_GROUP, tm), lambda i: (0, i)),
            pl.BlockSpec((1, d), lambda i: (0, 0)),
            pl.BlockSpec((1, d), lambda i: (0, 0)),
        ],
        out_specs=pl.BlockSpec((tm, d), lambda i: (i, 0)),
        out_shape=jax.ShapeDtypeStruct((t, d), F32),
        compiler_params=pltpu.CompilerParams(
            dimension_semantics=("parallel",), vmem_limit_bytes=VMEM_LIMIT),
        name="combine_postnorm",
    )(h1, ys, pos, wts, g.reshape(1, d), b.reshape(1, d))


def kernel(x, emb_ln_g, emb_ln_b, w_in, na_rpb, hg_lb, hg_norm_g, w_proj_a, w_proj_b, w_out, ln1_g, ln1_b,
           w_router_group, b_router_group, w_router_expert, b_router_expert, w_gate, w_up, w_down, ln2_g, ln2_b):
    batch, seq, d = x.shape
    depth = w_in.shape[0]
    t = batch * seq
    na_w = na_rpb.shape[1] * NA_HEAD_DIM
    hg_w = hg_norm_g.shape[1]
    hg_heads = hg_w // HG_DK
    alpha = float((2 * depth) ** 0.25)
    lb_all = jnp.cumsum(jax.nn.softmax(hg_lb.astype(F32), axis=1), axis=1)

    assert depth == 1, depth
    l = 0
    x2d = x.reshape(t, d)
    h0, na, hg, rest = _inproj(x2d, emb_ln_g, emb_ln_b, w_in[l].astype(BF16), lb_all[:, l], n_na=3 * na_w, hg_w=hg_w,
                           q_cols=na_w, q_scale=float(NA_HEAD_DIM ** -0.5 * LOG2_E))
    bias = _na_bias_tables(na_rpb[l], seq // GRID_W)
    a = _na(na.reshape(batch, seq, 3 * na_w), bias, batch=batch, seq=seq).reshape(t, na_w)
    o2 = _hgrn(hg, rest, batch=batch, seq=seq, heads=hg_heads)

    wr_t = jnp.zeros((LANES, d), F32)
    wr_t = wr_t.at[0:N_GROUPS].set(w_router_group[l].T.astype(F32))
    wr_t = wr_t.at[EXPERTS_PER_GROUP:EXPERTS_PER_GROUP + N_EXPERTS].set(w_router_expert[l].T.astype(F32))
    br = jnp.zeros((LANES,), F32)
    br = br.at[0:N_GROUPS].set(b_router_group[l].astype(F32))
    br = br.at[EXPERTS_PER_GROUP:EXPERTS_PER_GROUP + N_EXPERTS].set(b_router_expert[l].astype(F32))
    br_col = jnp.broadcast_to(br[:, None], (LANES, LANES))

    h1, pos, route_w, xs, tile_chunks = _merge(
        a, o2, rest, h0, hg_norm_g[l], w_proj_a[l].astype(BF16), w_proj_b[l].astype(BF16),
        w_out[l].astype(BF16), ln1_g[l], ln1_b[l], wr_t, br_col, heads=hg_heads, alpha=alpha)

    plan = _chunk_plan(tile_chunks[:, 0].reshape(t // MOE_TM, N_EXPERTS))
    ys = _experts(xs, *plan, w_gate[l], w_up[l], w_down[l])
    h2 = _final(h1, ys, pos, route_w, ln2_g[l], ln2_b[l], alpha=alpha)
    return h2.reshape(batch, seq, d)
```

```python
import functools

import numpy as np
import jax
import jax.numpy as jnp
from jax import lax
from jax.experimental import pallas as pl
from jax.experimental.pallas import tpu as pltpu

F32 = jnp.float32
BF16 = jnp.bfloat16

GRID_W = 64
NA_HEAD_DIM = 64
NA_KH = 8
NA_KW = 16
HG_DK = 128
N_GROUPS = 4
EXPERTS_PER_GROUP = 8
N_EXPERTS = N_GROUPS * EXPERTS_PER_GROUP
TOP_K = 2
LN_EPS = 1e-5
RMS_EPS = 1e-6

LANES = 128
SUBLANES = 8
VMEM_LIMIT = 56 * 1024 * 1024
NEG_BIG = -1e30
LOG2_E = 1.4426950408889634

NA_QROWS = 4
NA_KROWS = NA_QROWS + NA_KH
HG_CHUNK = 128
HG_LEVELS = 7
HG_PAIR = 2
MOE_TM = 512
MOE_CHUNK = 8
MOE_CAP = TOP_K * MOE_TM + N_EXPERTS * MOE_CHUNK
MOE_BM = 512
MOE_CPB = MOE_BM // MOE_CHUNK
MOE_DMA_GROUP = 8


def _ln(x, g, b):
    mu = jnp.mean(x, axis=-1, keepdims=True)
    xc = x - mu
    var = jnp.mean(xc * xc, axis=-1, keepdims=True)
    return xc * lax.rsqrt(var + LN_EPS) * g + b


def _sigmoid(x):
    return 0.5 * jnp.tanh(0.5 * x) + 0.5


def _inproj_kernel(x_ref, g_ref, b_ref, w_ref, lb_ref, h0_ref, na_ref, hg_ref, rest_ref, *,
                   n_na, hg_w, q_cols, q_scale):
    h0 = _ln(x_ref[...], g_ref[...], b_ref[...])
    h0_ref[...] = h0
    h = h0.astype(BF16)
    d_in = w_ref.shape[1]
    cn = hg_w
    for c0 in range(0, d_in, cn):
        u = jnp.dot(h, w_ref[:, c0:c0 + cn], preferred_element_type=F32)
        if c0 < q_cols:
            u = u * q_scale
        if c0 < n_na:
            na_ref[:, c0:c0 + cn] = u.astype(na_ref.dtype)
        elif c0 == n_na:
            hg_ref[:, 0:cn] = (u * _sigmoid(u)).astype(hg_ref.dtype)
        elif c0 < n_na + 3 * hg_w:
            d = (c0 - n_na) // hg_w - 1
            lb = lb_ref[d]
            forget = lb + (1.0 - lb) * _sigmoid(u)
            logf = jnp.log(forget)
            hi = logf.astype(BF16)
            hg_ref[:, (1 + d) * cn:(2 + d) * cn] = (1.0 - forget).astype(hg_ref.dtype)
            hg_ref[:, (3 + 2 * d) * cn:(4 + 2 * d) * cn] = hi
            hg_ref[:, (4 + 2 * d) * cn:(5 + 2 * d) * cn] = (logf - hi.astype(F32)).astype(BF16)
        else:
            o = c0 - n_na - 3 * hg_w
            rest_ref[:, o:o + cn] = u.astype(rest_ref.dtype)


def _inproj(x2d, g, b, w_bf16, lb, *, n_na, hg_w, q_cols, q_scale, tm=512):
    t, d = x2d.shape
    d_in = w_bf16.shape[1]
    n_rest = d_in - n_na - 3 * hg_w
    kern = functools.partial(_inproj_kernel, n_na=n_na, hg_w=hg_w, q_cols=q_cols, q_scale=q_scale)
    return pl.pallas_call(
        kern,
        grid=(t // tm,),
        in_specs=[
            pl.BlockSpec((tm, d), lambda i: (i, 0)),
            pl.BlockSpec((1, d), lambda i: (0, 0)),
            pl.BlockSpec((1, d), lambda i: (0, 0)),
            pl.BlockSpec((d, d_in), lambda i: (0, 0), pipeline_mode=pl.Buffered(1)),
            pl.BlockSpec((2, 1, hg_w), lambda i: (0, 0, 0)),
        ],
        out_specs=[
            pl.BlockSpec((tm, d), lambda i: (i, 0)),
            pl.BlockSpec((tm, n_na), lambda i: (i, 0)),
            pl.BlockSpec((tm, 7 * hg_w), lambda i: (i, 0)),
            pl.BlockSpec((tm, n_rest), lambda i: (i, 0)),
        ],
        out_shape=[
            jax.ShapeDtypeStruct((t, d), F32),
            jax.ShapeDtypeStruct((t, n_na), BF16),
            jax.ShapeDtypeStruct((t, 7 * hg_w), BF16),
            jax.ShapeDtypeStruct((t, n_rest), BF16),
        ],
        compiler_params=pltpu.CompilerParams(
            dimension_semantics=("parallel",), vmem_limit_bytes=VMEM_LIMIT),
        name="ln_inproj",
    )(x2d, g.reshape(1, d), b.reshape(1, d), w_bf16, lb.reshape(2, 1, hg_w))


def _na_bias_tables(rpb, rows):
    nq = NA_QROWS * GRID_W
    nk = NA_KROWS * GRID_W
    qi = np.arange(nq)
    kj = np.arange(nk)
    qc = (qi % GRID_W)[:, None]
    kc = (kj % GRID_W)[None, :]
    cs = np.clip(qc - NA_KW // 2, 0, GRID_W - NA_KW)
    valid_c = (kc >= cs) & (kc < cs + NA_KW)
    n_rel_r, n_rel_c = 2 * NA_KH - 1, 2 * NA_KW - 1
    wc = np.arange(GRID_W)
    cidx = np.clip(wc[None, :] - wc[:, None] + NA_KW - 1, 0, n_rel_c - 1)
    onehot_c = (cidx[None] == np.arange(n_rel_c)[:, None, None]).astype(np.float32)
    ridx_all, valid_all = [], []
    for r0 in (0, NA_QROWS, rows - NA_QROWS):
        start = int(np.clip(r0 - NA_KH // 2, 0, rows - NA_KROWS))
        qr = (r0 + qi // GRID_W)[:, None]
        kr = (start + kj // GRID_W)[None, :]
        rs = np.clip(qr - NA_KH // 2, 0, rows - NA_KH)
        valid_all.append((kr >= rs) & (kr < rs + NA_KH) & valid_c)
        qr_s = r0 + np.arange(NA_QROWS)[:, None]
        kr_s = start + np.arange(NA_KROWS)[None, :]
        ridx_all.append(np.clip(kr_s - qr_s + NA_KH - 1, 0, n_rel_r - 1))
    valid = np.stack(valid_all)
    rows_sel = rpb.astype(F32)[:, np.stack(ridx_all), :]
    bias = jnp.einsum('hcqkj,jxy->chqxky', rows_sel, onehot_c, precision=lax.Precision.HIGHEST)
    bias = bias.reshape(3, rpb.shape[0], nq, nk) * LOG2_E
    return jnp.where(valid[:, None], bias, NEG_BIG)


def _na_kernel(q_ref, k_ref, v_ref, bias_ref, o_ref, kt_ref, *, rows):
    kt_ref[...] = k_ref[0].T
    nq = NA_QROWS * GRID_W
    nk = NA_KROWS * GRID_W
    nrb = rows // NA_QROWS
    half = nq // 2
    lane = lax.broadcasted_iota(jnp.int32, (1, LANES), 1)
    first = lane < NA_HEAD_DIM
    work = [(hh, r0) for hh in range(2) for r0 in (0, half)]

    def row_block(rb, carry):
        start = jnp.clip(rb * NA_QROWS - NA_KH // 2, 0, rows - NA_KROWS)
        kstart = pl.multiple_of(start * GRID_W, NA_QROWS * GRID_W)
        qstart = pl.multiple_of(rb * nq, nq)
        cls = jnp.where(rb == 0, 0, jnp.where(rb == nrb - 1, 2, 1))
        kwin_t = kt_ref[:, pl.ds(kstart, nk)]
        vwin = v_ref[0, pl.ds(kstart, nk), :]
        q = q_ref[0, pl.ds(qstart, nq), :]
        zero = jnp.zeros_like(q)
        qs = [jnp.where(first, q, zero), jnp.where(first, zero, q)]
        ss = [jnp.dot(qs[hh][r0:r0 + half], kwin_t, preferred_element_type=F32)
              + bias_ref[cls, 0, hh, r0:r0 + half, :] for hh, r0 in work]
        ps = [jnp.exp2(s - jnp.max(s, axis=-1, keepdims=True)) for s in ss]
        ls = [jnp.sum(p, axis=-1, keepdims=True) for p in ps]
        os = [jnp.dot(p.astype(BF16), vwin, preferred_element_type=F32) / l for p, l in zip(ps, ls)]
        o_ref[0, pl.ds(qstart, nq), :] = jnp.where(first, jnp.concatenate(os[0:2], axis=0),
                                                   jnp.concatenate(os[2:4], axis=0)).astype(o_ref.dtype)
        return carry

    lax.fori_loop(0, nrb, row_block, 0)


def _na(qkv, bias, *, batch, seq):
    rows = seq // GRID_W
    width = qkv.shape[-1] // 3
    pairs = width // LANES
    nq = NA_QROWS * GRID_W
    nk = NA_KROWS * GRID_W
    bias5 = bias.reshape(3, pairs, 2, nq, nk)
    return pl.pallas_call(
        functools.partial(_na_kernel, rows=rows),
        grid=(pairs, batch),
        in_specs=[
            pl.BlockSpec((1, seq, LANES), lambda p, b: (b, 0, p)),
            pl.BlockSpec((1, seq, LANES), lambda p, b: (b, 0, pairs + p)),
            pl.BlockSpec((1, seq, LANES), lambda p, b: (b, 0, 2 * pairs + p)),
            pl.BlockSpec((3, 1, 2, nq, nk), lambda p, b: (0, p, 0, 0, 0)),
        ],
        out_specs=pl.BlockSpec((1, seq, LANES), lambda p, b: (b, 0, p)),
        out_shape=jax.ShapeDtypeStruct((batch, seq, width), BF16),
        scratch_shapes=[pltpu.VMEM((LANES, seq), BF16)],
        compiler_params=pltpu.CompilerParams(
            dimension_semantics=("parallel", "parallel"), vmem_limit_bytes=VMEM_LIMIT),
        name="neighbourhood_attention",
    )(qkv, qkv, qkv, bias5)


def _hg_tables():
    c, nl = HG_CHUNK, HG_LEVELS
    a = np.zeros((2, c * (nl + 1) + 8, c), np.float32)
    lvl = np.full((2, c, c), -1, np.int32)
    for rev in (0, 1):
        for t in range(c):
            if rev == 0:
                a[rev, t, :t + 1] = 1.0
            else:
                a[rev, t, t:] = 1.0
            lvl[rev, t, t] = nl
            for s in range(c):
                if (s < t and rev == 0) or (s > t and rev == 1):
                    lvl[rev, t, s] = int(np.floor(np.log2(t ^ s)))
            for l in range(nl):
                m = 1 << l
                mid = (t & ~(2 * m - 1)) + m
                upper = (t >> l) & 1
                row = c * (l + 1) + t
                if rev == 0:
                    lo, hi = (mid, t) if upper else (t + 1, mid - 1)
                else:
                    lo, hi = (mid, t - 1) if upper else (t, mid - 1)
                if hi >= lo:
                    a[rev, row, lo:hi + 1] = 1.0
        a[rev, c * (nl + 1):, :] = 1.0
    return a, lvl


def _hg_kernel(q_ref, k_ref, up_ref, dn_ref, ghi_ref, glo_ref, i_ref, a_ref, lvl_ref, o_ref, st_ref, xt_ref, *,
               heads, n_chunks):
    rev = pl.program_id(1)
    c, nl = HG_CHUNK, HG_LEVELS

    @pl.when(pl.program_id(2) == 0)
    def _():
        st_ref[...] = jnp.zeros_like(st_ref)

    a_mat = a_ref[0]
    lvl = lvl_ref[0]
    w = heads * LANES
    sub = lax.broadcasted_iota(jnp.int32, (SUBLANES, w), 0)
    nt = (((1,), (1,)), ((), ()))
    tn = (((0,), (0,)), ((), ()))
    head_cols = [slice(h * LANES, (h + 1) * LANES) for h in range(heads)]

    def level_operand(l, up, dn):
        m = 1 << l
        if m >= SUBLANES:
            return jnp.concatenate([(up if k & 1 else dn)[k * m:(k + 1) * m] for k in range(c // m)], axis=0)
        upper = ((sub >> l) & 1) == 1
        shape3 = (c // SUBLANES, SUBLANES, w)
        return jnp.where(upper[None], up.reshape(shape3), dn.reshape(shape3)).reshape(c, w)

    def chunk(j, u):
        cj = jnp.where(rev == 1, n_chunks - 1 - j, j)
        r0 = pl.multiple_of(cj * c, c)
        rows = pl.ds(r0, c)
        qb = q_ref[rows, :]
        kb = k_ref[rows, :]
        q = qb.astype(F32)
        kk = kb.astype(F32)
        up = up_ref[rows, :].astype(F32)
        dn = dn_ref[rows, :].astype(F32)
        v = i_ref[rows, :]
        e = jnp.dot(a_mat, jnp.concatenate([ghi_ref[rows, :], glo_ref[rows, :]], axis=0),
                    preferred_element_type=F32)
        b = e[0:c]
        btot = e[c * (nl + 1):c * (nl + 1) + 1]
        qd = (q * jnp.exp(b)).astype(BF16)
        khat = (kk * jnp.exp(btot - b)).astype(BF16)
        decay = jnp.exp(btot)
        xs = [(level_operand(l, up, dn) * jnp.exp(e[c * (l + 1):c * (l + 2)])).astype(BF16) for l in range(nl)]
        xt_ref[u, nl] = kb.T
        attns = [jnp.where(lvl == nl, jnp.dot(qb[:, hc], xt_ref[u, nl, hc, :], preferred_element_type=F32), 0.0)
                 for hc in head_cols]
        for l in range(nl):
            x = xs[l]
            xt_ref[u, l] = x.T
            attns = [jnp.where(lvl == l, jnp.dot(x[:, hc], xt_ref[u, l, hc, :], preferred_element_type=F32), at)
                     for hc, at in zip(head_cols, attns)]
        intra = [jnp.dot(at.astype(BF16), v[:, hc], preferred_element_type=F32)
                 for hc, at in zip(head_cols, attns)]
        update = [lax.dot_general(khat[:, hc], v[:, hc], tn, preferred_element_type=F32) for hc in head_cols]
        decay_cols = [jnp.broadcast_to(decay[:, hc], (LANES, LANES)).T for hc in head_cols]
        return rows, qd, decay_cols, intra, update

    def chunk_pair(jp, carry):
        parts = [chunk(jp * HG_PAIR + u, u) for u in range(HG_PAIR)]
        sts = [st_ref[h] for h in range(heads)]
        for rows, qd, decay_cols, intra, update in parts:
            for h, hc in enumerate(head_cols):
                o_ref[0, rows, hc] = (intra[h] + jnp.dot(qd[:, hc], sts[h].astype(BF16),
                                                         preferred_element_type=F32)).astype(o_ref.dtype)
            sts = [st * dc + up_h for st, dc, up_h in zip(sts, decay_cols, update)]
        for h in range(heads):
            st_ref[h] = sts[h]
        return carry

    lax.fori_loop(0, n_chunks // HG_PAIR, chunk_pair, 0)


def _hgrn(hg, rest, *, batch, seq, heads, cb=1024):
    t = batch * seq
    w = heads * LANES
    nsb = seq // cb
    a_np, lvl_np = _hg_tables()
    a_np = np.concatenate([a_np, a_np], axis=-1)
    a_mat = jnp.asarray(a_np, BF16)
    lvl = jnp.asarray(lvl_np)

    def rowblk(b, d, s):
        return b * nsb + jnp.where(d == 1, nsb - 1 - s, s)

    def col(fn):
        return pl.BlockSpec((cb, w), lambda b, d, s: (rowblk(b, d, s), fn(d)))

    return pl.pallas_call(
        functools.partial(_hg_kernel, heads=heads, n_chunks=cb // HG_CHUNK),
        grid=(batch, 2, nsb),
        in_specs=[
            col(lambda d: 0),
            col(lambda d: 1 + d),
            col(lambda d: 2 * d),
            col(lambda d: 1 - d),
            col(lambda d: 3 + 2 * d),
            col(lambda d: 4 + 2 * d),
            col(lambda d: 0),
            pl.BlockSpec((1,) + a_np.shape[1:], lambda b, d, s: (d, 0, 0)),
            pl.BlockSpec((1,) + lvl_np.shape[1:], lambda b, d, s: (d, 0, 0)),
        ],
        out_specs=pl.BlockSpec((1, cb, w), lambda b, d, s: (d, rowblk(b, d, s), 0)),
        out_shape=jax.ShapeDtypeStruct((2, t, w), BF16),
        scratch_shapes=[pltpu.VMEM((heads, LANES, LANES), F32),
                        pltpu.VMEM((HG_PAIR, HG_LEVELS + 1, w, HG_CHUNK), BF16)],
        compiler_params=pltpu.CompilerParams(
            dimension_semantics=("parallel", "parallel", "arbitrary"), vmem_limit_bytes=VMEM_LIMIT),
        name="hgrn2_scan",
    )(hg, hg, hg, hg, hg, hg, rest, a_mat, lvl)


def _first_index_of_max(x, idx):
    m = jnp.max(x, axis=0, keepdims=True)
    i = jnp.min(jnp.where(x == m, idx, EXPERTS_PER_GROUP), axis=0, keepdims=True)
    return m, i


def _merge_kernel(a_ref, o2_ref, g_ref, ga_ref, gb_ref, h0_ref, ng_ref,
                  wa_ref, wb_ref, wo_ref, ln1g_ref, ln1b_ref, wrh_ref, wrl_ref, br_ref,
                  h1_ref, ri_ref, rw_ref, xs_ref, pc_ref, ht_ref, *, heads, alpha):
    o = o2_ref[0].astype(F32) + o2_ref[1].astype(F32)
    gate = g_ref[...].astype(F32)
    parts = []
    for h in range(heads):
        cols = slice(h * LANES, (h + 1) * LANES)
        oh = o[:, cols]
        ms = jnp.mean(oh * oh, axis=-1, keepdims=True)
        gh = gate[:, cols]
        parts.append((oh * lax.rsqrt(ms + RMS_EPS) * ng_ref[:, cols] * (gh * _sigmoid(gh))).astype(BF16))
    c = jnp.concatenate(parts, axis=-1)
    pa = jnp.dot(a_ref[...], wa_ref[...], preferred_element_type=F32)
    pb = jnp.dot(c, wb_ref[...], preferred_element_type=F32)
    merged = _sigmoid(ga_ref[...].astype(F32)) * pa + _sigmoid(gb_ref[...].astype(F32)) * pb
    y = jnp.dot(merged.astype(BF16), wo_ref[...], preferred_element_type=F32)
    h1 = _ln(alpha * h0_ref[...] + y, ln1g_ref[...], ln1b_ref[...])
    h1_ref[...] = h1

    d = h1.shape[1]
    h_hi = h1.astype(BF16)
    h_lo = (h1 - h_hi.astype(F32)).astype(BF16)
    ht_ref[0:d] = h_hi.T
    ht_ref[d:2 * d] = h_lo.T
    logits = (jnp.dot(wrh_ref[...], ht_ref[0:d], preferred_element_type=F32)
              + jnp.dot(wrh_ref[...], ht_ref[d:2 * d], preferred_element_type=F32)
              + jnp.dot(wrl_ref[...], ht_ref[0:d], preferred_element_type=F32) + br_ref[:, 0:1])
    n = logits.shape[1]
    idx = lax.broadcasted_iota(jnp.int32, (EXPERTS_PER_GROUP, n), 0)
    glog = jnp.where(idx < N_GROUPS, logits[0:EXPERTS_PER_GROUP], -jnp.inf)
    gmax, gsel = _first_index_of_max(glog, idx)
    gw = 1.0 / jnp.sum(jnp.exp(glog - gmax), axis=0, keepdims=True)
    elog = logits[EXPERTS_PER_GROUP:2 * EXPERTS_PER_GROUP]
    for gi in range(1, N_GROUPS):
        lo = EXPERTS_PER_GROUP * (gi + 1)
        elog = jnp.where(gsel == gi, logits[lo:lo + EXPERTS_PER_GROUP], elog)
    ee = jnp.exp(elog - jnp.max(elog, axis=0, keepdims=True))
    p = ee / jnp.sum(ee, axis=0, keepdims=True)
    p1, i1 = _first_index_of_max(p, idx)
    p2, i2 = _first_index_of_max(jnp.where(idx == i1, -1.0, p), idx)
    denom = p1 + p2
    e1 = gsel * EXPERTS_PER_GROUP + i1
    e2 = gsel * EXPERTS_PER_GROUP + i2
    rw_ref[...] = jnp.where(idx == 0, gw * p1 / denom, jnp.where(idx == 1, gw * p2 / denom, 0.0))

    ex = lax.broadcasted_iota(jnp.int32, (N_EXPERTS, n), 0)
    m1 = ex == e1
    m2 = ex == e2
    member = jnp.where(m1, 1.0, jnp.where(m2, 1.0, 0.0))
    before = (lax.broadcasted_iota(jnp.int32, (n, n), 0) < lax.broadcasted_iota(jnp.int32, (n, n), 1))
    rank = jnp.dot(member.astype(BF16), jnp.where(before, 1.0, 0.0).astype(BF16), preferred_element_type=F32)
    count = jnp.sum(member, axis=1, keepdims=True)
    chunks = jnp.floor((count + (MOE_CHUNK - 1)) * (1.0 / MOE_CHUNK))
    chunks_b = jnp.broadcast_to(chunks, (N_EXPERTS, LANES))
    lower = (lax.broadcasted_iota(jnp.int32, (N_EXPERTS, N_EXPERTS), 1)
             < lax.broadcasted_iota(jnp.int32, (N_EXPERTS, N_EXPERTS), 0))
    seg = jnp.dot(jnp.where(lower, 1.0, 0.0), chunks_b, precision=lax.Precision.HIGHEST,
                  preferred_element_type=F32)
    posmat = seg[:, 0:1] * MOE_CHUNK + rank
    pos1 = jnp.sum(jnp.where(m1, posmat, 0.0), axis=0, keepdims=True).astype(jnp.int32)
    pos2 = jnp.sum(jnp.where(m2, posmat, 0.0), axis=0, keepdims=True).astype(jnp.int32)
    ri_ref[...] = jnp.where(idx == 0, pos1, jnp.where(idx == 1, pos2, 0))
    pc_ref[...] = chunks_b
    slot = lax.broadcasted_iota(jnp.int32, (xs_ref.shape[0], n), 0)
    perm = jnp.where(slot == pos1, 1.0, jnp.where(slot == pos2, 1.0, 0.0)).astype(BF16)
    xs_ref[...] = jnp.dot(perm, h_hi, preferred_element_type=F32)


def _merge(a, o2, rest, h0, norm_g, wa, wb, wo, ln1g, ln1b, wr_t, br_col, *, heads, alpha):
    t, d = h0.shape
    w = heads * LANES
    wa_w = a.shape[1]
    tm = MOE_TM
    n_tiles = t // tm
    wr_hi = wr_t.astype(BF16)
    wr_lo = (wr_t - wr_hi.astype(F32)).astype(BF16)
    full = lambda shape: pl.BlockSpec(shape, lambda i: (0,) * len(shape))
    return pl.pallas_call(
        functools.partial(_merge_kernel, heads=heads, alpha=alpha),
        grid=(t // tm,),
        in_specs=[
            pl.BlockSpec((tm, wa_w), lambda i: (i, 0)),
            pl.BlockSpec((2, tm, w), lambda i: (0, i, 0)),
            pl.BlockSpec((tm, w), lambda i: (i, 1)),
            pl.BlockSpec((tm, d), lambda i: (i, 1)),
            pl.BlockSpec((tm, d), lambda i: (i, 2)),
            pl.BlockSpec((tm, d), lambda i: (i, 0)),
            full((1, w)),
            full(wa.shape), full(wb.shape), full(wo.shape),
            full((1, d)), full((1, d)),
            full(wr_t.shape), full(wr_t.shape), full(br_col.shape),
        ],
        out_specs=[
            pl.BlockSpec((tm, d), lambda i: (i, 0)),
            pl.BlockSpec((EXPERTS_PER_GROUP, tm), lambda i: (0, i)),
            pl.BlockSpec((EXPERTS_PER_GROUP, tm), lambda i: (0, i)),
            pl.BlockSpec((MOE_CAP, d), lambda i: (i, 0)),
            pl.BlockSpec((N_EXPERTS, LANES), lambda i: (i, 0)),
        ],
        out_shape=[
            jax.ShapeDtypeStruct((t, d), F32),
            jax.ShapeDtypeStruct((EXPERTS_PER_GROUP, t), jnp.int32),
            jax.ShapeDtypeStruct((EXPERTS_PER_GROUP, t), F32),
            jax.ShapeDtypeStruct((n_tiles * MOE_CAP, d), F32),
            jax.ShapeDtypeStruct((n_tiles * N_EXPERTS, LANES), F32),
        ],
        scratch_shapes=[pltpu.VMEM((2 * d, tm), BF16)],
        compiler_params=pltpu.CompilerParams(
            dimension_semantics=("parallel",), vmem_limit_bytes=VMEM_LIMIT),
        name="merge_outproj_router",
    )(a, o2, rest, rest, rest, h0, norm_g.reshape(1, w),
      wa, wb, wo, ln1g.reshape(1, d), ln1b.reshape(1, d), wr_hi, wr_lo, br_col)


def _expert_kernel(be_ref, nv_ref, cur_ref, nxt_ref, xy_in, wg_ref, wu_ref, wd_ref, xy_out,
                   xbuf, ybuf, wg_s, wu_s, wd_s, gsem, ssem):
    i = pl.program_id(0)
    n = pl.num_programs(0)
    slot = i % 2
    nv = nv_ref[i]

    def gather_copy(tbl_ref, s, buf_slot):
        src = pl.multiple_of(tbl_ref[0, 0, s] * MOE_CHUNK, MOE_CHUNK)
        dst = pl.multiple_of(s * MOE_CHUNK, MOE_CHUNK)
        return pltpu.make_async_copy(xy_in.at[pl.ds(src, MOE_CHUNK), :],
                                     xbuf.at[buf_slot, pl.ds(dst, MOE_CHUNK), :], gsem.at[buf_slot])

    def scatter_copy(s, buf_slot):
        src = pl.multiple_of(s * MOE_CHUNK, MOE_CHUNK)
        dst = pl.multiple_of(cur_ref[0, 0, s] * MOE_CHUNK, MOE_CHUNK)
        return pltpu.make_async_copy(ybuf.at[buf_slot, pl.ds(src, MOE_CHUNK), :],
                                     xy_out.at[pl.ds(dst, MOE_CHUNK), :], ssem.at[buf_slot])

    def for_each_chunk(count, fn):
        for g0 in range(0, MOE_CPB, MOE_DMA_GROUP):
            @pl.when(count >= g0 + MOE_DMA_GROUP)
            def _():
                for s in range(g0, g0 + MOE_DMA_GROUP):
                    fn(s)

        def body(s, c):
            fn(s)
            return c
        lax.fori_loop(count // MOE_DMA_GROUP * MOE_DMA_GROUP, count, body, 0)

    def start_gathers(tbl_ref, buf_slot):
        for s in range(MOE_CPB):
            gather_copy(tbl_ref, s, buf_slot).start()

    def wait_scatters(count, buf_slot):
        for_each_chunk(count, lambda s: scatter_copy(0, buf_slot).wait())

    @pl.when((i == 0) & (nv > 0))
    def _():
        start_gathers(cur_ref, slot)

    nxt = jnp.minimum(i + 1, n - 1)

    @pl.when((i + 1 < n) & (nv_ref[nxt] > 0))
    def _():
        start_gathers(nxt_ref, 1 - slot)

    @pl.when(i >= 2)
    def _():
        wait_scatters(nv_ref[jnp.maximum(i - 2, 0)], slot)

    @pl.when(nv > 0)
    def _():
        for _ in range(MOE_CPB):
            gather_copy(cur_ref, 0, slot).wait()

        @pl.when((i == 0) | (be_ref[i] != be_ref[jnp.maximum(i - 1, 0)]))
        def _():
            wg_s[...] = wg_ref[0].astype(BF16)
            wu_s[...] = wu_ref[0].astype(BF16)
            wd_s[...] = wd_ref[0].astype(BF16)

        x = xbuf[slot].astype(BF16)
        gate = jnp.dot(x, wg_s[...], preferred_element_type=F32)
        up = jnp.dot(x, wu_s[...], preferred_element_type=F32)
        hid = (gate * _sigmoid(gate) * up).astype(BF16)
        ybuf[slot] = jnp.dot(hid, wd_s[...], preferred_element_type=F32)

        for_each_chunk(nv, lambda s: scatter_copy(s, slot).start())

    @pl.when(i == n - 1)
    def _():
        @pl.when(i >= 1)
        def _():
            wait_scatters(nv_ref[jnp.maximum(i - 1, 0)], 1 - slot)
        wait_scatters(nv, slot)


def _experts(xy, block_expert, block_nvalid, table, wg, wu, wd):
    rows, d = xy.shape
    n_blocks = table.shape[0]
    ne, _, dff = wg.shape
    return pl.pallas_call(
        _expert_kernel,
        grid_spec=pltpu.PrefetchScalarGridSpec(
            num_scalar_prefetch=2,
            grid=(n_blocks,),
            in_specs=[
                pl.BlockSpec((1, 1, MOE_CPB), lambda i, be, nv: (i, 0, 0), memory_space=pltpu.SMEM),
                pl.BlockSpec((1, 1, MOE_CPB), lambda i, be, nv: (jnp.minimum(i + 1, n_blocks - 1), 0, 0),
                             memory_space=pltpu.SMEM),
                pl.BlockSpec(memory_space=pl.ANY),
                pl.BlockSpec((1, d, dff), lambda i, be, nv: (be[i], 0, 0)),
                pl.BlockSpec((1, d, dff), lambda i, be, nv: (be[i], 0, 0)),
                pl.BlockSpec((1, dff, d), lambda i, be, nv: (be[i], 0, 0)),
            ],
            out_specs=pl.BlockSpec(memory_space=pl.ANY),
            scratch_shapes=[
                pltpu.VMEM((2, MOE_BM, d), F32),
                pltpu.VMEM((2, MOE_BM, d), F32),
                pltpu.VMEM((d, dff), BF16),
                pltpu.VMEM((d, dff), BF16),
                pltpu.VMEM((dff, d), BF16),
                pltpu.SemaphoreType.DMA((2,)),
                pltpu.SemaphoreType.DMA((2,)),
            ],
        ),
        out_shape=jax.ShapeDtypeStruct((rows, d), F32),
        input_output_aliases={4: 0},
        compiler_params=pltpu.CompilerParams(
            dimension_semantics=("arbitrary",), vmem_limit_bytes=VMEM_LIMIT),
        name="expert_ffn",
    )(block_expert, block_nvalid, table, table, xy, wg, wu, wd)


def _chunk_plan(tile_chunks):
    n_tiles = tile_chunks.shape[0]
    cap_chunks = MOE_CAP // MOE_CHUNK
    max_tile_chunks = (TOP_K * MOE_TM + N_EXPERTS * (MOE_CHUNK - 1)) // MOE_CHUNK
    n_blocks = -(-n_tiles * max_tile_chunks // MOE_CPB) + N_EXPERTS
    pc = tile_chunks.astype(jnp.int32)
    seg = jnp.cumsum(pc, axis=1) - pc
    tile_end = jnp.cumsum(pc.T, axis=1)
    tile_beg = tile_end - pc.T
    tot = tile_end[:, -1]
    nblk = (tot + MOE_CPB - 1) // MOE_CPB
    bend = jnp.cumsum(nblk)
    bbeg = bend - nblk
    bidx = jnp.arange(n_blocks, dtype=jnp.int32)
    be = jnp.minimum(jnp.sum((bend[None, :] <= bidx[:, None]).astype(jnp.int32), axis=1), N_EXPERTS - 1)
    onehot_e = (be[:, None] == jnp.arange(N_EXPERTS, dtype=jnp.int32)[None, :]).astype(jnp.int32)
    pick = lambda vec: jnp.sum(onehot_e * vec[None, :], axis=1)
    q = ((bidx - pick(bbeg)) * MOE_CPB)[:, None] + jnp.arange(MOE_CPB, dtype=jnp.int32)[None, :]
    valid = (q < pick(tot)[:, None]) & (bidx < bend[-1])[:, None]
    rows_of = lambda tbl: jnp.sum(onehot_e[:, :, None] * tbl[None, :, :], axis=1)
    blk_end = rows_of(tile_end)
    base = jnp.arange(n_tiles, dtype=jnp.int32)[None, :] * cap_chunks + seg.T - tile_beg
    blk_base = rows_of(base)
    tile = jnp.sum((blk_end[:, None, :] <= q[:, :, None]).astype(jnp.int32), axis=2)
    tile = jnp.minimum(tile, n_tiles - 1)
    onehot_t = tile[:, :, None] == jnp.arange(n_tiles, dtype=jnp.int32)[None, None, :]
    src = q + jnp.sum(jnp.where(onehot_t, blk_base[:, None, :], 0), axis=2)
    src = jnp.where(valid, src, src[:, 0:1])
    nvalid = jnp.sum(valid, axis=1).astype(jnp.int32)
    src = jnp.where(nvalid[:, None] > 0, src, 0).astype(jnp.int32)
    return be.astype(jnp.int32), nvalid, src.reshape(n_blocks, 1, MOE_CPB)


def _final_kernel(h1_ref, ys_ref, pos_ref, w_ref, g_ref, b_ref, o_ref, *, alpha):
    cap = ys_ref.shape[0]
    n = h1_ref.shape[0]
    slot = lax.broadcasted_iota(jnp.int32, (cap, n), 0)
    first = slot == pos_ref[0:1, :]
    second = slot == pos_ref[1:2, :]
    row_w = jnp.sum(jnp.where(first, w_ref[0:1, :], jnp.where(second, w_ref[1:2, :], 0.0)), axis=1, keepdims=True)
    unsort = jnp.where(first, 1.0, jnp.where(second, 1.0, 0.0)).astype(BF16)
    ys = (ys_ref[...] * row_w).astype(BF16)
    moe = lax.dot_general(unsort, ys, (((0,), (0,)), ((), ())), preferred_element_type=F32)
    o_ref[...] = _ln(alpha * h1_ref[...] + moe, g_ref[...], b_ref[...])


def _final(h1, ys, pos, wts, g, b, *, alpha):
    t, d = h1.shape
    tm = MOE_TM
    return pl.pallas_call(
        functools.partial(_final_kernel, alpha=alpha),
        grid=(t // tm,),
        in_specs=[
            pl.BlockSpec((tm, d), lambda i: (i, 0)),
            pl.BlockSpec((MOE_CAP, d), lambda i: (i, 0)),
            pl.BlockSpec((EXPERTS_PER_GROUP, tm), lambda i: (0, i)),
            pl.BlockSpec((EXPERTS_PER_GROUP, tm), lambda i: (0, i)),
            pl.BlockSpec((1, d), lambda i: (0, 0)),
            pl.BlockSpec((1, d), lambda i: (0, 0)),
        ],
        out_specs=pl.BlockSpec((tm, d), lambda i: (i, 0)),
        out_shape=jax.ShapeDtypeStruct((t, d), F32),
        compiler_params=pltpu.CompilerParams(
            dimension_semantics=("parallel",), vmem_limit_bytes=VMEM_LIMIT),
        name="combine_postnorm",
    )(h1, ys, pos, wts, g.reshape(1, d), b.reshape(1, d))


def kernel(x, emb_ln_g, emb_ln_b, w_in, na_rpb, hg_lb, hg_norm_g, w_proj_a, w_proj_b, w_out, ln1_g, ln1_b,
           w_router_group, b_router_group, w_router_expert, b_router_expert, w_gate, w_up, w_down, ln2_g, ln2_b):
    batch, seq, d = x.shape
    depth = w_in.shape[0]
    t = batch * seq
    na_w = na_rpb.shape[1] * NA_HEAD_DIM
    hg_w = hg_norm_g.shape[1]
    hg_heads = hg_w // HG_DK
    alpha = float((2 * depth) ** 0.25)
    lb_all = jnp.cumsum(jax.nn.softmax(hg_lb.astype(F32), axis=1), axis=1)

    assert depth == 1, depth
    l = 0
    x2d = x.reshape(t, d)
    h0, na, hg, rest = _inproj(x2d, emb_ln_g, emb_ln_b, w_in[l].astype(BF16), lb_all[:, l], n_na=3 * na_w, hg_w=hg_w,
                           q_cols=na_w, q_scale=float(NA_HEAD_DIM ** -0.5 * LOG2_E))
    bias = _na_bias_tables(na_rpb[l], seq // GRID_W)
    a = _na(na.reshape(batch, seq, 3 * na_w), bias, batch=batch, seq=seq).reshape(t, na_w)
    o2 = _hgrn(hg, rest, batch=batch, seq=seq, heads=hg_heads)

    wr_t = jnp.zeros((LANES, d), F32)
    wr_t = wr_t.at[0:N_GROUPS].set(w_router_group[l].T.astype(F32))
    wr_t = wr_t.at[EXPERTS_PER_GROUP:EXPERTS_PER_GROUP + N_EXPERTS].set(w_router_expert[l].T.astype(F32))
    br = jnp.zeros((LANES,), F32)
    br = br.at[0:N_GROUPS].set(b_router_group[l].astype(F32))
    br = br.at[EXPERTS_PER_GROUP:EXPERTS_PER_GROUP + N_EXPERTS].set(b_router_expert[l].astype(F32))
    br_col = jnp.broadcast_to(br[:, None], (LANES, LANES))

    h1, pos, route_w, xs, tile_chunks = _merge(
        a, o2, rest, h0, hg_norm_g[l], w_proj_a[l].astype(BF16), w_proj_b[l].astype(BF16),
        w_out[l].astype(BF16), ln1_g[l], ln1_b[l], wr_t, br_col, heads=hg_heads, alpha=alpha)

    plan = _chunk_plan(tile_chunks[:, 0].reshape(t // MOE_TM, N_EXPERTS))
    ys = _experts(xs, *plan, w_gate[l], w_up[l], w_down[l])
    h2 = _final(h1, ys, pos, route_w, ln2_g[l], ln2_b[l], alpha=alpha)
    return h2.reshape(batch, seq, d)
```

```python
import functools

import numpy as np
import jax
import jax.numpy as jnp
from jax import lax
from jax.experimental import pallas as pl
from jax.experimental.pallas import tpu as pltpu

F32 = jnp.float32
BF16 = jnp.bfloat16

GRID_W = 64
NA_HEAD_DIM = 64
NA_KH = 8
NA_KW = 16
HG_DK = 128
N_GROUPS = 4
EXPERTS_PER_GROUP = 8
N_EXPERTS = N_GROUPS * EXPERTS_PER_GROUP
TOP_K = 2
LN_EPS = 1e-5
RMS_EPS = 1e-6

LANES = 128
SUBLANES = 8
VMEM_LIMIT = 56 * 1024 * 1024
NEG_BIG = -1e30
LOG2_E = 1.4426950408889634

NA_QROWS = 4
NA_KROWS = NA_QROWS + NA_KH
HG_CHUNK = 128
HG_LEVELS = 7
HG_GROUP = 4
MOE_TM = 512
MOE_CHUNK = 8
MOE_CAP = TOP_K * MOE_TM + N_EXPERTS * MOE_CHUNK
MOE_BM = 512
MOE_CPB = MOE_BM // MOE_CHUNK
MOE_DMA_GROUP = 8


def _ln(x, g, b):
    mu = jnp.mean(x, axis=-1, keepdims=True)
    xc = x - mu
    var = jnp.mean(xc * xc, axis=-1, keepdims=True)
    return xc * lax.rsqrt(var + LN_EPS) * g + b


def _sigmoid(x):
    return 0.5 * jnp.tanh(0.5 * x) + 0.5


def _inproj_kernel(x_ref, g_ref, b_ref, w_ref, lb_ref, h0_ref, na_ref, hg_ref, rest_ref, *,
                   n_na, hg_w, q_cols, q_scale):
    h0 = _ln(x_ref[...], g_ref[...], b_ref[...])
    h0_ref[...] = h0
    h = h0.astype(BF16)
    d_in = w_ref.shape[1]
    cn = hg_w
    for c0 in range(0, d_in, cn):
        u = jnp.dot(h, w_ref[:, c0:c0 + cn], preferred_element_type=F32)
        if c0 < q_cols:
            u = u * q_scale
        if c0 < n_na:
            na_ref[:, c0:c0 + cn] = u.astype(na_ref.dtype)
        elif c0 == n_na:
            hg_ref[:, 0:cn] = (u * _sigmoid(u)).astype(hg_ref.dtype)
        elif c0 < n_na + 3 * hg_w:
            d = (c0 - n_na) // hg_w - 1
            lb = lb_ref[d]
            forget = lb + (1.0 - lb) * _sigmoid(u)
            logf = jnp.log(forget)
            hi = logf.astype(BF16)
            hg_ref[:, (1 + d) * cn:(2 + d) * cn] = (1.0 - forget).astype(hg_ref.dtype)
            hg_ref[:, (3 + 2 * d) * cn:(4 + 2 * d) * cn] = hi
            hg_ref[:, (4 + 2 * d) * cn:(5 + 2 * d) * cn] = (logf - hi.astype(F32)).astype(BF16)
        else:
            o = c0 - n_na - 3 * hg_w
            rest_ref[:, o:o + cn] = u.astype(rest_ref.dtype)


def _inproj(x2d, g, b, w_bf16, lb, *, n_na, hg_w, q_cols, q_scale, tm=512):
    t, d = x2d.shape
    d_in = w_bf16.shape[1]
    n_rest = d_in - n_na - 3 * hg_w
    kern = functools.partial(_inproj_kernel, n_na=n_na, hg_w=hg_w, q_cols=q_cols, q_scale=q_scale)
    return pl.pallas_call(
        kern,
        grid=(t // tm,),
        in_specs=[
            pl.BlockSpec((tm, d), lambda i: (i, 0)),
            pl.BlockSpec((1, d), lambda i: (0, 0)),
            pl.BlockSpec((1, d), lambda i: (0, 0)),
            pl.BlockSpec((d, d_in), lambda i: (0, 0), pipeline_mode=pl.Buffered(1)),
            pl.BlockSpec((2, 1, hg_w), lambda i: (0, 0, 0)),
        ],
        out_specs=[
            pl.BlockSpec((tm, d), lambda i: (i, 0)),
            pl.BlockSpec((tm, n_na), lambda i: (i, 0)),
            pl.BlockSpec((tm, 7 * hg_w), lambda i: (i, 0)),
            pl.BlockSpec((tm, n_rest), lambda i: (i, 0)),
        ],
        out_shape=[
            jax.ShapeDtypeStruct((t, d), F32),
            jax.ShapeDtypeStruct((t, n_na), BF16),
            jax.ShapeDtypeStruct((t, 7 * hg_w), BF16),
            jax.ShapeDtypeStruct((t, n_rest), BF16),
        ],
        compiler_params=pltpu.CompilerParams(
            dimension_semantics=("parallel",), vmem_limit_bytes=VMEM_LIMIT),
        name="ln_inproj",
    )(x2d, g.reshape(1, d), b.reshape(1, d), w_bf16, lb.reshape(2, 1, hg_w))


def _na_bias_tables(rpb, rows):
    nq = NA_QROWS * GRID_W
    nk = NA_KROWS * GRID_W
    qi = np.arange(nq)
    kj = np.arange(nk)
    qc = (qi % GRID_W)[:, None]
    kc = (kj % GRID_W)[None, :]
    cs = np.clip(qc - NA_KW // 2, 0, GRID_W - NA_KW)
    valid_c = (kc >= cs) & (kc < cs + NA_KW)
    n_rel_r, n_rel_c = 2 * NA_KH - 1, 2 * NA_KW - 1
    wc = np.arange(GRID_W)
    cidx = np.clip(wc[None, :] - wc[:, None] + NA_KW - 1, 0, n_rel_c - 1)
    onehot_c = (cidx[None] == np.arange(n_rel_c)[:, None, None]).astype(np.float32)
    ridx_all, valid_all = [], []
    for r0 in (0, NA_QROWS, rows - NA_QROWS):
        start = int(np.clip(r0 - NA_KH // 2, 0, rows - NA_KROWS))
        qr = (r0 + qi // GRID_W)[:, None]
        kr = (start + kj // GRID_W)[None, :]
        rs = np.clip(qr - NA_KH // 2, 0, rows - NA_KH)
        valid_all.append((kr >= rs) & (kr < rs + NA_KH) & valid_c)
        qr_s = r0 + np.arange(NA_QROWS)[:, None]
        kr_s = start + np.arange(NA_KROWS)[None, :]
        ridx_all.append(np.clip(kr_s - qr_s + NA_KH - 1, 0, n_rel_r - 1))
    valid = np.stack(valid_all)
    rows_sel = rpb.astype(F32)[:, np.stack(ridx_all), :]
    bias = jnp.einsum('hcqkj,jxy->chqxky', rows_sel, onehot_c, precision=lax.Precision.HIGHEST)
    bias = bias.reshape(3, rpb.shape[0], nq, nk) * LOG2_E
    return jnp.where(valid[:, None], bias, NEG_BIG)


def _na_kernel(q_ref, k_ref, v_ref, bias_ref, o_ref, kt_ref, *, rows):
    kt_ref[...] = k_ref[0].T
    nq = NA_QROWS * GRID_W
    nk = NA_KROWS * GRID_W
    nrb = rows // NA_QROWS
    half = nq // 2
    lane = lax.broadcasted_iota(jnp.int32, (1, LANES), 1)
    first = lane < NA_HEAD_DIM
    work = [(hh, r0) for hh in range(2) for r0 in (0, half)]

    def row_block(rb, carry):
        start = jnp.clip(rb * NA_QROWS - NA_KH // 2, 0, rows - NA_KROWS)
        kstart = pl.multiple_of(start * GRID_W, NA_QROWS * GRID_W)
        qstart = pl.multiple_of(rb * nq, nq)
        cls = jnp.where(rb == 0, 0, jnp.where(rb == nrb - 1, 2, 1))
        kwin_t = kt_ref[:, pl.ds(kstart, nk)]
        vwin = v_ref[0, pl.ds(kstart, nk), :]
        q = q_ref[0, pl.ds(qstart, nq), :]
        zero = jnp.zeros_like(q)
        qs = [jnp.where(first, q, zero), jnp.where(first, zero, q)]
        ss = [jnp.dot(qs[hh][r0:r0 + half], kwin_t, preferred_element_type=F32)
              + bias_ref[cls, 0, hh, r0:r0 + half, :] for hh, r0 in work]
        ps = [jnp.exp2(s - jnp.max(s, axis=-1, keepdims=True)) for s in ss]
        ls = [jnp.sum(p, axis=-1, keepdims=True) for p in ps]
        os = [jnp.dot(p.astype(BF16), vwin, preferred_element_type=F32) / l for p, l in zip(ps, ls)]
        o_ref[0, pl.ds(qstart, nq), :] = jnp.where(first, jnp.concatenate(os[0:2], axis=0),
                                                   jnp.concatenate(os[2:4], axis=0)).astype(o_ref.dtype)
        return carry

    lax.fori_loop(0, nrb, row_block, 0)


def _na(qkv, bias, *, batch, seq):
    rows = seq // GRID_W
    width = qkv.shape[-1] // 3
    pairs = width // LANES
    nq = NA_QROWS * GRID_W
    nk = NA_KROWS * GRID_W
    bias5 = bias.reshape(3, pairs, 2, nq, nk)
    return pl.pallas_call(
        functools.partial(_na_kernel, rows=rows),
        grid=(pairs, batch),
        in_specs=[
            pl.BlockSpec((1, seq, LANES), lambda p, b: (b, 0, p)),
            pl.BlockSpec((1, seq, LANES), lambda p, b: (b, 0, pairs + p)),
            pl.BlockSpec((1, seq, LANES), lambda p, b: (b, 0, 2 * pairs + p)),
            pl.BlockSpec((3, 1, 2, nq, nk), lambda p, b: (0, p, 0, 0, 0)),
        ],
        out_specs=pl.BlockSpec((1, seq, LANES), lambda p, b: (b, 0, p)),
        out_shape=jax.ShapeDtypeStruct((batch, seq, width), BF16),
        scratch_shapes=[pltpu.VMEM((LANES, seq), BF16)],
        compiler_params=pltpu.CompilerParams(
            dimension_semantics=("parallel", "parallel"), vmem_limit_bytes=VMEM_LIMIT),
        name="neighbourhood_attention",
    )(qkv, qkv, qkv, bias5)


def _hg_tables():
    c, nl = HG_CHUNK, HG_LEVELS
    a = np.zeros((2, c * (nl + 1) + 8, c), np.float32)
    lvl = np.full((2, c, c), -1, np.int32)
    for rev in (0, 1):
        for t in range(c):
            if rev == 0:
                a[rev, t, :t + 1] = 1.0
            else:
                a[rev, t, t:] = 1.0
            lvl[rev, t, t] = nl
            for s in range(c):
                if (s < t and rev == 0) or (s > t and rev == 1):
                    lvl[rev, t, s] = int(np.floor(np.log2(t ^ s)))
            for l in range(nl):
                m = 1 << l
                mid = (t & ~(2 * m - 1)) + m
                upper = (t >> l) & 1
                row = c * (l + 1) + t
                if rev == 0:
                    lo, hi = (mid, t) if upper else (t + 1, mid - 1)
                else:
                    lo, hi = (mid, t - 1) if upper else (t, mid - 1)
                if hi >= lo:
                    a[rev, row, lo:hi + 1] = 1.0
        a[rev, c * (nl + 1):, :] = 1.0
    return a, lvl


def _hg_kernel(q_ref, k_ref, up_ref, dn_ref, ghi_ref, glo_ref, i_ref, a_ref, lvl_ref, o_ref, st_ref, xt_ref, *,
               heads, n_chunks):
    rev = pl.program_id(1)
    c, nl = HG_CHUNK, HG_LEVELS

    @pl.when(pl.program_id(2) == 0)
    def _():
        st_ref[...] = jnp.zeros_like(st_ref)

    a_mat = a_ref[0]
    lvl = lvl_ref[0]
    w = heads * LANES
    sub = lax.broadcasted_iota(jnp.int32, (SUBLANES, w), 0)
    tn = (((0,), (0,)), ((), ()))
    head_cols = [slice(h * LANES, (h + 1) * LANES) for h in range(heads)]

    def level_operand(l, up, dn):
        m = 1 << l
        if m >= SUBLANES:
            return jnp.concatenate([(up if k & 1 else dn)[k * m:(k + 1) * m] for k in range(c // m)], axis=0)
        upper = ((sub >> l) & 1) == 1
        shape3 = (c // SUBLANES, SUBLANES, w)
        return jnp.where(upper[None], up.reshape(shape3), dn.reshape(shape3)).reshape(c, w)

    def chunk(j, u):
        cj = jnp.where(rev == 1, n_chunks - 1 - j, j)
        r0 = pl.multiple_of(cj * c, c)
        rows = pl.ds(r0, c)
        qb = q_ref[rows, :]
        kb = k_ref[rows, :]
        q = qb.astype(F32)
        kk = kb.astype(F32)
        up = up_ref[rows, :].astype(F32)
        dn = dn_ref[rows, :].astype(F32)
        v = i_ref[rows, :]
        e = jnp.dot(a_mat, jnp.concatenate([ghi_ref[rows, :], glo_ref[rows, :]], axis=0),
                    preferred_element_type=F32)
        b = e[0:c]
        btot = e[c * (nl + 1):c * (nl + 1) + 1]
        qd = (q * jnp.exp(b)).astype(BF16)
        khat = (kk * jnp.exp(btot - b)).astype(BF16)
        decay = jnp.exp(btot)
        xs = [(level_operand(l, up, dn) * jnp.exp(e[c * (l + 1):c * (l + 2)])).astype(BF16) for l in range(nl)]
        xt_ref[u, nl] = kb.T
        attns = [jnp.where(lvl == nl, jnp.dot(qb[:, hc], xt_ref[u, nl, hc, :], preferred_element_type=F32), 0.0)
                 for hc in head_cols]
        for l in range(nl):
            x = xs[l]
            xt_ref[u, l] = x.T
            attns = [jnp.where(lvl == l, jnp.dot(x[:, hc], xt_ref[u, l, hc, :], preferred_element_type=F32), at)
                     for hc, at in zip(head_cols, attns)]
        intra = [jnp.dot(at.astype(BF16), v[:, hc], preferred_element_type=F32)
                 for hc, at in zip(head_cols, attns)]
        update = [lax.dot_general(khat[:, hc], v[:, hc], tn, preferred_element_type=F32) for hc in head_cols]
        decay_cols = [jnp.broadcast_to(decay[:, hc], (LANES, LANES)).T for hc in head_cols]
        return rows, qd, decay_cols, intra, update

    def chunk_group(jp, carry):
        parts = [chunk(jp * HG_GROUP + u, u) for u in range(HG_GROUP)]
        sts = [st_ref[h] for h in range(heads)]
        for rows, qd, decay_cols, intra, update in parts:
            for h, hc in enumerate(head_cols):
                o_ref[0, rows, hc] = (intra[h] + jnp.dot(qd[:, hc], sts[h].astype(BF16),
                                                         preferred_element_type=F32)).astype(o_ref.dtype)
            sts = [st * dc + up_h for st, dc, up_h in zip(sts, decay_cols, update)]
        for h in range(heads):
            st_ref[h] = sts[h]
        return carry

    lax.fori_loop(0, n_chunks // HG_GROUP, chunk_group, 0)


def _hgrn(hg, rest, *, batch, seq, heads, cb=1024):
    t = batch * seq
    w = heads * LANES
    nsb = seq // cb
    a_np, lvl_np = _hg_tables()
    a_np = np.concatenate([a_np, a_np], axis=-1)
    a_mat = jnp.asarray(a_np, BF16)
    lvl = jnp.asarray(lvl_np)

    def rowblk(b, d, s):
        return b * nsb + jnp.where(d == 1, nsb - 1 - s, s)

    def col(fn):
        return pl.BlockSpec((cb, w), lambda b, d, s: (rowblk(b, d, s), fn(d)))

    return pl.pallas_call(
        functools.partial(_hg_kernel, heads=heads, n_chunks=cb // HG_CHUNK),
        grid=(batch, 2, nsb),
        in_specs=[
            col(lambda d: 0),
            col(lambda d: 1 + d),
            col(lambda d: 2 * d),
            col(lambda d: 1 - d),
            col(lambda d: 3 + 2 * d),
            col(lambda d: 4 + 2 * d),
            col(lambda d: 0),
            pl.BlockSpec((1,) + a_np.shape[1:], lambda b, d, s: (d, 0, 0)),
            pl.BlockSpec((1,) + lvl_np.shape[1:], lambda b, d, s: (d, 0, 0)),
        ],
        out_specs=pl.BlockSpec((1, cb, w), lambda b, d, s: (d, rowblk(b, d, s), 0)),
        out_shape=jax.ShapeDtypeStruct((2, t, w), BF16),
        scratch_shapes=[pltpu.VMEM((heads, LANES, LANES), F32),
                        pltpu.VMEM((HG_GROUP, HG_LEVELS + 1, w, HG_CHUNK), BF16)],
        compiler_params=pltpu.CompilerParams(
            dimension_semantics=("parallel", "parallel", "arbitrary"), vmem_limit_bytes=VMEM_LIMIT),
        name="hgrn2_scan",
    )(hg, hg, hg, hg, hg, hg, rest, a_mat, lvl)


def _first_index_of_max(x, idx):
    m = jnp.max(x, axis=0, keepdims=True)
    i = jnp.min(jnp.where(x == m, idx, EXPERTS_PER_GROUP), axis=0, keepdims=True)
    return m, i


def _merge_kernel(a_ref, o2_ref, g_ref, ga_ref, gb_ref, h0_ref, ng_ref,
                  wa_ref, wb_ref, wo_ref, ln1g_ref, ln1b_ref, wrh_ref, wrl_ref, br_ref,
                  h1_ref, ri_ref, rw_ref, xs_ref, pc_ref, ht_ref, *, heads, alpha):
    o = o2_ref[0].astype(F32) + o2_ref[1].astype(F32)
    gate = g_ref[...].astype(F32)
    parts = []
    for h in range(heads):
        cols = slice(h * LANES, (h + 1) * LANES)
        oh = o[:, cols]
        ms = jnp.mean(oh * oh, axis=-1, keepdims=True)
        gh = gate[:, cols]
        parts.append((oh * lax.rsqrt(ms + RMS_EPS) * ng_ref[:, cols] * (gh * _sigmoid(gh))).astype(BF16))
    c = jnp.concatenate(parts, axis=-1)
    pa = jnp.dot(a_ref[...], wa_ref[...], preferred_element_type=F32)
    pb = jnp.dot(c, wb_ref[...], preferred_element_type=F32)
    merged = _sigmoid(ga_ref[...].astype(F32)) * pa + _sigmoid(gb_ref[...].astype(F32)) * pb
    y = jnp.dot(merged.astype(BF16), wo_ref[...], preferred_element_type=F32)
    h1 = _ln(alpha * h0_ref[...] + y, ln1g_ref[...], ln1b_ref[...])
    h1_ref[...] = h1

    d = h1.shape[1]
    h_hi = h1.astype(BF16)
    h_lo = (h1 - h_hi.astype(F32)).astype(BF16)
    ht_ref[0:d] = h_hi.T
    ht_ref[d:2 * d] = h_lo.T
    logits = (jnp.dot(wrh_ref[...], ht_ref[0:d], preferred_element_type=F32)
              + jnp.dot(wrh_ref[...], ht_ref[d:2 * d], preferred_element_type=F32)
              + jnp.dot(wrl_ref[...], ht_ref[0:d], preferred_element_type=F32) + br_ref[:, 0:1])
    n = logits.shape[1]
    idx = lax.broadcasted_iota(jnp.int32, (EXPERTS_PER_GROUP, n), 0)
    glog = jnp.where(idx < N_GROUPS, logits[0:EXPERTS_PER_GROUP], -jnp.inf)
    gmax, gsel = _first_index_of_max(glog, idx)
    gw = 1.0 / jnp.sum(jnp.exp(glog - gmax), axis=0, keepdims=True)
    elog = logits[EXPERTS_PER_GROUP:2 * EXPERTS_PER_GROUP]
    for gi in range(1, N_GROUPS):
        lo = EXPERTS_PER_GROUP * (gi + 1)
        elog = jnp.where(gsel == gi, logits[lo:lo + EXPERTS_PER_GROUP], elog)
    ee = jnp.exp(elog - jnp.max(elog, axis=0, keepdims=True))
    p = ee / jnp.sum(ee, axis=0, keepdims=True)
    p1, i1 = _first_index_of_max(p, idx)
    p2, i2 = _first_index_of_max(jnp.where(idx == i1, -1.0, p), idx)
    denom = p1 + p2
    e1 = gsel * EXPERTS_PER_GROUP + i1
    e2 = gsel * EXPERTS_PER_GROUP + i2
    rw_ref[...] = jnp.where(idx == 0, gw * p1 / denom, jnp.where(idx == 1, gw * p2 / denom, 0.0))

    ex = lax.broadcasted_iota(jnp.int32, (N_EXPERTS, n), 0)
    m1 = ex == e1
    m2 = ex == e2
    member = jnp.where(m1, 1.0, jnp.where(m2, 1.0, 0.0))
    before = (lax.broadcasted_iota(jnp.int32, (n, n), 0) < lax.broadcasted_iota(jnp.int32, (n, n), 1))
    rank = jnp.dot(member.astype(BF16), jnp.where(before, 1.0, 0.0).astype(BF16), preferred_element_type=F32)
    count = jnp.sum(member, axis=1, keepdims=True)
    chunks = jnp.floor((count + (MOE_CHUNK - 1)) * (1.0 / MOE_CHUNK))
    chunks_b = jnp.broadcast_to(chunks, (N_EXPERTS, LANES))
    lower = (lax.broadcasted_iota(jnp.int32, (N_EXPERTS, N_EXPERTS), 1)
             < lax.broadcasted_iota(jnp.int32, (N_EXPERTS, N_EXPERTS), 0))
    seg = jnp.dot(jnp.where(lower, 1.0, 0.0), chunks_b, precision=lax.Precision.HIGHEST,
                  preferred_element_type=F32)
    posmat = seg[:, 0:1] * MOE_CHUNK + rank
    pos1 = jnp.sum(jnp.where(m1, posmat, 0.0), axis=0, keepdims=True).astype(jnp.int32)
    pos2 = jnp.sum(jnp.where(m2, posmat, 0.0), axis=0, keepdims=True).astype(jnp.int32)
    ri_ref[...] = jnp.where(idx == 0, pos1, jnp.where(idx == 1, pos2, 0))
    pc_ref[...] = chunks_b
    slot = lax.broadcasted_iota(jnp.int32, (xs_ref.shape[0], n), 0)
    perm = jnp.where(slot == pos1, 1.0, jnp.where(slot == pos2, 1.0, 0.0)).astype(BF16)
    xs_ref[...] = jnp.dot(perm, h_hi, preferred_element_type=F32)


def _merge(a, o2, rest, h0, norm_g, wa, wb, wo, ln1g, ln1b, wr_t, br_col, *, heads, alpha):
    t, d = h0.shape
    w = heads * LANES
    wa_w = a.shape[1]
    tm = MOE_TM
    n_tiles = t // tm
    wr_hi = wr_t.astype(BF16)
    wr_lo = (wr_t - wr_hi.astype(F32)).astype(BF16)
    full = lambda shape: pl.BlockSpec(shape, lambda i: (0,) * len(shape))
    return pl.pallas_call(
        functools.partial(_merge_kernel, heads=heads, alpha=alpha),
        grid=(t // tm,),
        in_specs=[
            pl.BlockSpec((tm, wa_w), lambda i: (i, 0)),
            pl.BlockSpec((2, tm, w), lambda i: (0, i, 0)),
            pl.BlockSpec((tm, w), lambda i: (i, 1)),
            pl.BlockSpec((tm, d), lambda i: (i, 1)),
            pl.BlockSpec((tm, d), lambda i: (i, 2)),
            pl.BlockSpec((tm, d), lambda i: (i, 0)),
            full((1, w)),
            full(wa.shape), full(wb.shape), full(wo.shape),
            full((1, d)), full((1, d)),
            full(wr_t.shape), full(wr_t.shape), full(br_col.shape),
        ],
        out_specs=[
            pl.BlockSpec((tm, d), lambda i: (i, 0)),
            pl.BlockSpec((EXPERTS_PER_GROUP, tm), lambda i: (0, i)),
            pl.BlockSpec((EXPERTS_PER_GROUP, tm), lambda i: (0, i)),
            pl.BlockSpec((MOE_CAP, d), lambda i: (i, 0)),
            pl.BlockSpec((N_EXPERTS, LANES), lambda i: (i, 0)),
        ],
        out_shape=[
            jax.ShapeDtypeStruct((t, d), F32),
            jax.ShapeDtypeStruct((EXPERTS_PER_GROUP, t), jnp.int32),
            jax.ShapeDtypeStruct((EXPERTS_PER_GROUP, t), F32),
            jax.ShapeDtypeStruct((n_tiles * MOE_CAP, d), F32),
            jax.ShapeDtypeStruct((n_tiles * N_EXPERTS, LANES), F32),
        ],
        scratch_shapes=[pltpu.VMEM((2 * d, tm), BF16)],
        compiler_params=pltpu.CompilerParams(
            dimension_semantics=("parallel",), vmem_limit_bytes=VMEM_LIMIT),
        name="merge_outproj_router",
    )(a, o2, rest, rest, rest, h0, norm_g.reshape(1, w),
      wa, wb, wo, ln1g.reshape(1, d), ln1b.reshape(1, d), wr_hi, wr_lo, br_col)


def _expert_kernel(be_ref, nv_ref, cur_ref, nxt_ref, xy_in, wg_ref, wu_ref, wd_ref, xy_out,
                   xbuf, ybuf, wg_s, wu_s, wd_s, gsem, ssem):
    i = pl.program_id(0)
    n = pl.num_programs(0)
    slot = i % 2
    nv = nv_ref[i]

    def gather_copy(tbl_ref, s, buf_slot):
        src = pl.multiple_of(tbl_ref[0, 0, s] * MOE_CHUNK, MOE_CHUNK)
        dst = pl.multiple_of(s * MOE_CHUNK, MOE_CHUNK)
        return pltpu.make_async_copy(xy_in.at[pl.ds(src, MOE_CHUNK), :],
                                     xbuf.at[buf_slot, pl.ds(dst, MOE_CHUNK), :], gsem.at[buf_slot])

    def scatter_copy(s, buf_slot):
        src = pl.multiple_of(s * MOE_CHUNK, MOE_CHUNK)
        dst = pl.multiple_of(cur_ref[0, 0, s] * MOE_CHUNK, MOE_CHUNK)
        return pltpu.make_async_copy(ybuf.at[buf_slot, pl.ds(src, MOE_CHUNK), :],
                                     xy_out.at[pl.ds(dst, MOE_CHUNK), :], ssem.at[buf_slot])

    def for_each_chunk(count, fn):
        for g0 in range(0, MOE_CPB, MOE_DMA_GROUP):
            @pl.when(count >= g0 + MOE_DMA_GROUP)
            def _():
                for s in range(g0, g0 + MOE_DMA_GROUP):
                    fn(s)

        def body(s, c):
            fn(s)
            return c
        lax.fori_loop(count // MOE_DMA_GROUP * MOE_DMA_GROUP, count, body, 0)

    def start_gathers(tbl_ref, buf_slot):
        for s in range(MOE_CPB):
            gather_copy(tbl_ref, s, buf_slot).start()

    def wait_scatters(count, buf_slot):
        for_each_chunk(count, lambda s: scatter_copy(0, buf_slot).wait())

    @pl.when((i == 0) & (nv > 0))
    def _():
        start_gathers(cur_ref, slot)

    nxt = jnp.minimum(i + 1, n - 1)

    @pl.when((i + 1 < n) & (nv_ref[nxt] > 0))
    def _():
        start_gathers(nxt_ref, 1 - slot)

    @pl.when(i >= 2)
    def _():
        wait_scatters(nv_ref[jnp.maximum(i - 2, 0)], slot)

    @pl.when(nv > 0)
    def _():
        for _ in range(MOE_CPB):
            gather_copy(cur_ref, 0, slot).wait()

        @pl.when((i == 0) | (be_ref[i] != be_ref[jnp.maximum(i - 1, 0)]))
        def _():
            wg_s[...] = wg_ref[0].astype(BF16)
            wu_s[...] = wu_ref[0].astype(BF16)
            wd_s[...] = wd_ref[0].astype(BF16)

        x = xbuf[slot].astype(BF16)
        gate = jnp.dot(x, wg_s[...], preferred_element_type=F32)
        up = jnp.dot(x, wu_s[...], preferred_element_type=F32)
        hid = (gate * _sigmoid(gate) * up).astype(BF16)
        ybuf[slot] = jnp.dot(hid, wd_s[...], preferred_element_type=F32)

        for_each_chunk(nv, lambda s: scatter_copy(s, slot).start())

    @pl.when(i == n - 1)
    def _():
        @pl.when(i >= 1)
        def _():
            wait_scatters(nv_ref[jnp.maximum(i - 1, 0)], 1 - slot)
        wait_scatters(nv, slot)


def _experts(xy, block_expert, block_nvalid, table, wg, wu, wd):
    rows, d = xy.shape
    n_blocks = table.shape[0]
    ne, _, dff = wg.shape
    return pl.pallas_call(
        _expert_kernel,
        grid_spec=pltpu.PrefetchScalarGridSpec(
            num_scalar_prefetch=2,
            grid=(n_blocks,),
            in_specs=[
                pl.BlockSpec((1, 1, MOE_CPB), lambda i, be, nv: (i, 0, 0), memory_space=pltpu.SMEM),
                pl.BlockSpec((1, 1, MOE_CPB), lambda i, be, nv: (jnp.minimum(i + 1, n_blocks - 1), 0, 0),
                             memory_space=pltpu.SMEM),
                pl.BlockSpec(memory_space=pl.ANY),
                pl.BlockSpec((1, d, dff), lambda i, be, nv: (be[i], 0, 0)),
                pl.BlockSpec((1, d, dff), lambda i, be, nv: (be[i], 0, 0)),
                pl.BlockSpec((1, dff, d), lambda i, be, nv: (be[i], 0, 0)),
            ],
            out_specs=pl.BlockSpec(memory_space=pl.ANY),
            scratch_shapes=[
                pltpu.VMEM((2, MOE_BM, d), F32),
                pltpu.VMEM((2, MOE_BM, d), F32),
                pltpu.VMEM((d, dff), BF16),
                pltpu.VMEM((d, dff), BF16),
                pltpu.VMEM((dff, d), BF16),
                pltpu.SemaphoreType.DMA((2,)),
                pltpu.SemaphoreType.DMA((2,)),
            ],
        ),
        out_shape=jax.ShapeDtypeStruct((rows, d), F32),
        input_output_aliases={4: 0},
        compiler_params=pltpu.CompilerParams(
            dimension_semantics=("arbitrary",), vmem_limit_bytes=VMEM_LIMIT),
        name="expert_ffn",
    )(block_expert, block_nvalid, table, table, xy, wg, wu, wd)


def _chunk_plan(tile_chunks):
    n_tiles = tile_chunks.shape[0]
    cap_chunks = MOE_CAP // MOE_CHUNK
    max_tile_chunks = (TOP_K * MOE_TM + N_EXPERTS * (MOE_CHUNK - 1)) // MOE_CHUNK
    n_blocks = -(-n_tiles * max_tile_chunks // MOE_CPB) + N_EXPERTS
    pc = tile_chunks.astype(jnp.int32)
    seg = jnp.cumsum(pc, axis=1) - pc
    tile_end = jnp.cumsum(pc.T, axis=1)
    tile_beg = tile_end - pc.T
    tot = tile_end[:, -1]
    nblk = (tot + MOE_CPB - 1) // MOE_CPB
    bend = jnp.cumsum(nblk)
    bbeg = bend - nblk
    bidx = jnp.arange(n_blocks, dtype=jnp.int32)
    be = jnp.minimum(jnp.sum((bend[None, :] <= bidx[:, None]).astype(jnp.int32), axis=1), N_EXPERTS - 1)
    onehot_e = (be[:, None] == jnp.arange(N_EXPERTS, dtype=jnp.int32)[None, :]).astype(jnp.int32)
    pick = lambda vec: jnp.sum(onehot_e * vec[None, :], axis=1)
    q = ((bidx - pick(bbeg)) * MOE_CPB)[:, None] + jnp.arange(MOE_CPB, dtype=jnp.int32)[None, :]
    valid = (q < pick(tot)[:, None]) & (bidx < bend[-1])[:, None]
    rows_of = lambda tbl: jnp.sum(onehot_e[:, :, None] * tbl[None, :, :], axis=1)
    blk_end = rows_of(tile_end)
    base = jnp.arange(n_tiles, dtype=jnp.int32)[None, :] * cap_chunks + seg.T - tile_beg
    blk_base = rows_of(base)
    tile = jnp.sum((blk_end[:, None, :] <= q[:, :, None]).astype(jnp.int32), axis=2)
    tile = jnp.minimum(tile, n_tiles - 1)
    onehot_t = tile[:, :, None] == jnp.arange(n_tiles, dtype=jnp.int32)[None, None, :]
    src = q + jnp.sum(jnp.where(onehot_t, blk_base[:, None, :], 0), axis=2)
    src = jnp.where(valid, src, src[:, 0:1])
    nvalid = jnp.sum(valid, axis=1).astype(jnp.int32)
    src = jnp.where(nvalid[:, None] > 0, src, 0).astype(jnp.int32)
    return be.astype(jnp.int32), nvalid, src.reshape(n_blocks, 1, MOE_CPB)


def _final_kernel(h1_ref, ys_ref, pos_ref, w_ref, g_ref, b_ref, o_ref, *, alpha):
    cap = ys_ref.shape[0]
    n = h1_ref.shape[0]
    slot = lax.broadcasted_iota(jnp.int32, (cap, n), 0)
    first = slot == pos_ref[0:1, :]
    second = slot == pos_ref[1:2, :]
    row_w = jnp.sum(jnp.where(first, w_ref[0:1, :], jnp.where(second, w_ref[1:2, :], 0.0)), axis=1, keepdims=True)
    unsort = jnp.where(first, 1.0, jnp.where(second, 1.0, 0.0)).astype(BF16)
    ys = (ys_ref[...] * row_w).astype(BF16)
    moe = lax.dot_general(unsort, ys, (((0,), (0,)), ((), ())), preferred_element_type=F32)
    o_ref[...] = _ln(alpha * h1_ref[...] + moe, g_ref[...], b_ref[...])


def _final(h1, ys, pos, wts, g, b, *, alpha):
    t, d = h1.shape
    tm = MOE_TM
    return pl.pallas_call(
        functools.partial(_final_kernel, alpha=alpha),
        grid=(t // tm,),
        in_specs=[
            pl.BlockSpec((tm, d), lambda i: (i, 0)),
            pl.BlockSpec((MOE_CAP, d), lambda i: (i, 0)),
            pl.BlockSpec((EXPERTS_PER_GROUP, tm), lambda i: (0, i)),
            pl.BlockSpec((EXPERTS_PER_GROUP, tm), lambda i: (0, i)),
            pl.BlockSpec((1, d), lambda i: (0, 0)),
            pl.BlockSpec((1, d), lambda i: (0, 0)),
        ],
        out_specs=pl.BlockSpec((tm, d), lambda i: (i, 0)),
        out_shape=jax.ShapeDtypeStruct((t, d), F32),
        compiler_params=pltpu.CompilerParams(
            dimension_semantics=("parallel",), vmem_limit_bytes=VMEM_LIMIT),
        name="combine_postnorm",
    )(h1, ys, pos, wts, g.reshape(1, d), b.reshape(1, d))


def kernel(x, emb_ln_g, emb_ln_b, w_in, na_rpb, hg_lb, hg_norm_g, w_proj_a, w_proj_b, w_out, ln1_g, ln1_b,
           w_router_group, b_router_group, w_router_expert, b_router_expert, w_gate, w_up, w_down, ln2_g, ln2_b):
    batch, seq, d = x.shape
    depth = w_in.shape[0]
    t = batch * seq
    na_w = na_rpb.shape[1] * NA_HEAD_DIM
    hg_w = hg_norm_g.shape[1]
    hg_heads = hg_w // HG_DK
    alpha = float((2 * depth) ** 0.25)
    lb_all = jnp.cumsum(jax.nn.softmax(hg_lb.astype(F32), axis=1), axis=1)

    assert depth == 1, depth
    l = 0
    x2d = x.reshape(t, d)
    h0, na, hg, rest = _inproj(x2d, emb_ln_g, emb_ln_b, w_in[l].astype(BF16), lb_all[:, l], n_na=3 * na_w, hg_w=hg_w,
                           q_cols=na_w, q_scale=float(NA_HEAD_DIM ** -0.5 * LOG2_E))
    bias = _na_bias_tables(na_rpb[l], seq // GRID_W)
    a = _na(na.reshape(batch, seq, 3 * na_w), bias, batch=batch, seq=seq).reshape(t, na_w)
    o2 = _hgrn(hg, rest, batch=batch, seq=seq, heads=hg_heads)

    wr_t = jnp.zeros((LANES, d), F32)
    wr_t = wr_t.at[0:N_GROUPS].set(w_router_group[l].T.astype(F32))
    wr_t = wr_t.at[EXPERTS_PER_GROUP:EXPERTS_PER_GROUP + N_EXPERTS].set(w_router_expert[l].T.astype(F32))
    br = jnp.zeros((LANES,), F32)
    br = br.at[0:N_GROUPS].set(b_router_group[l].astype(F32))
    br = br.at[EXPERTS_PER_GROUP:EXPERTS_PER_GROUP + N_EXPERTS].set(b_router_expert[l].astype(F32))
    br_col = jnp.broadcast_to(br[:, None], (LANES, LANES))

    h1, pos, route_w, xs, tile_chunks = _merge(
        a, o2, rest, h0, hg_norm_g[l], w_proj_a[l].astype(BF16), w_proj_b[l].astype(BF16),
        w_out[l].astype(BF16), ln1_g[l], ln1_b[l], wr_t, br_col, heads=hg_heads, alpha=alpha)

    plan = _chunk_plan(tile_chunks[:, 0].reshape(t // MOE_TM, N_EXPERTS))
    ys = _experts(xs, *plan, w_gate[l], w_up[l], w_down[l])
    h2 = _final(h1, ys, pos, route_w, ln2_g[l], ln2_b[l], alpha=alpha)
    return h2.reshape(batch, seq, d)
```
